```python
import jax, jax.numpy as jnp
from jax import lax
import numpy as np

D_MODEL = 1024
BATCH = 2
SEQ = 8192
DEPTH = 4
DEC_BATCH = 128
DEC_SEQ = 1
PAST_LEN = 2048
PAGE_SIZE = 128

N_MIXERS = 2
N_HEADS = 16
HEAD_DIM = D_MODEL // N_HEADS
CONV_WIDTH = 3
D_FF = (7 * D_MODEL) // 2
N_EXPERTS = 8
TOP_K = 2
N_META = 16
Q_BLOCK = 128
EPS = 1e-6
SB_BIAS_INIT = -8.0
N_CONV_LAYERS = (DEPTH + 1) // 2
N_SB_LAYERS = DEPTH // 2
N_DENSE_LAYERS = (DEPTH + 1) // 2
N_MOE_LAYERS = DEPTH // 2

kernel_name = "hybrid_shortconv_stickbreaking_moe_step"


def rms_norm(x, g):
    xf = x.astype(jnp.float32)
    y = xf * lax.rsqrt(jnp.mean(xf * xf, axis=-1, keepdims=True) + EPS)
    return (y * g.astype(jnp.float32)).astype(x.dtype)


def short_conv_mixer(h, prev, w_in, conv_w, w_out):
    t = h.shape[1]
    b_gate, c_gate, val = jnp.split(h @ w_in, 3, axis=-1)
    u = c_gate * val
    full = jnp.concatenate([prev.astype(u.dtype), u], axis=1)
    conv = full[:, 0:t] * conv_w[0]
    for i in range(1, CONV_WIDTH):
        conv = conv + full[:, i:i + t] * conv_w[i]
    y = (b_gate * conv) @ w_out
    return y, full[:, -(CONV_WIDTH - 1):]


def qkv_heads(h, w_qkv):
    n, t, _ = h.shape
    qkv = (h @ w_qkv).reshape(n, t, 3, N_HEADS, HEAD_DIM)
    return qkv[:, :, 0], qkv[:, :, 1], qkv[:, :, 2]


def sb_attend(q, k, v, q_pos, k_pos, bias):
    z = jnp.einsum('bqhd,bkhd->bhqk', q.astype(jnp.float32), k.astype(jnp.float32)) * (HEAD_DIM ** -0.5)
    z = z + bias.astype(jnp.float32)[None, :, None, None]
    visible = k_pos[None, :] < q_pos[:, None]
    log_keep = jnp.where(visible, jax.nn.log_sigmoid(-z), 0.0)
    log_after = lax.cumsum(log_keep, axis=3, reverse=True) - log_keep
    a = jnp.where(visible, jnp.exp(jax.nn.log_sigmoid(z) + log_after), 0.0)
    return jnp.einsum('bhqk,bkhd->bqhd', a, v.astype(jnp.float32)).astype(q.dtype)


def sb_mixer_prompt(h, w_qkv, w_o, bias):
    n, t, _ = h.shape
    q, k, v = qkv_heads(h, w_qkv)
    n_blk = -(-t // Q_BLOCK)
    t_pad = n_blk * Q_BLOCK
    pad = ((0, 0), (0, t_pad - t), (0, 0), (0, 0))
    qp, kp, vp = jnp.pad(q, pad), jnp.pad(k, pad), jnp.pad(v, pad)
    pos = jnp.arange(t_pad, dtype=jnp.int32)
    q_blocks = qp.reshape(n, n_blk, Q_BLOCK, N_HEADS, HEAD_DIM).transpose(1, 0, 2, 3, 4)
    pos_blocks = pos.reshape(n_blk, Q_BLOCK)
    o = lax.map(lambda qb: sb_attend(qb[0], kp, vp, qb[1], pos, bias), (q_blocks, pos_blocks))
    o = o.transpose(1, 0, 2, 3, 4).reshape(n, t_pad, D_MODEL)[:, :t]
    return o @ w_o, k, v


def sb_mixer_sample(h, cache_k, cache_v, layer, page_table, w_qkv, w_o, bias):
    n, t, _ = h.shape
    q, k, v = qkv_heads(h, w_qkv)
    past = page_table.shape[1] * PAGE_SIZE
    k_past = cache_k[layer][page_table].reshape(n, past, N_HEADS, HEAD_DIM).astype(k.dtype)
    v_past = cache_v[layer][page_table].reshape(n, past, N_HEADS, HEAD_DIM).astype(v.dtype)
    keys = jnp.concatenate([k_past, k], axis=1)
    vals = jnp.concatenate([v_past, v], axis=1)
    q_pos = past + jnp.arange(t, dtype=jnp.int32)
    k_pos = jnp.arange(past + t, dtype=jnp.int32)
    o = sb_attend(q, keys, vals, q_pos, k_pos, bias).reshape(n, t, D_MODEL)
    return o @ w_o, k, v


def swiglu(h, wg, wu, wd):
    return (jax.nn.silu(h @ wg) * (h @ wu)) @ wd


def moe_ffn(h, router_w, wg, wu, wd):
    probs = jax.nn.softmax((h @ router_w).astype(jnp.float32), axis=-1)
    top_p, top_i = lax.top_k(probs, TOP_K)
    top_p = top_p / jnp.sum(top_p, axis=-1, keepdims=True)
    gates = jnp.sum(jax.nn.one_hot(top_i, N_EXPERTS, dtype=jnp.float32) * top_p[..., None], axis=-2).astype(h.dtype)
    out = gates[..., 0:1] * swiglu(h, wg[0], wu[0], wd[0])
    for e in range(1, N_EXPERTS):
        out = out + gates[..., e:e + 1] * swiglu(h, wg[e], wu[e], wd[e])
    return out


def setup_inputs(seed: int = 0) -> dict:
    key = jax.random.key(seed)
    ks = jax.random.split(key, 24)
    n_pages = PAST_LEN // PAGE_SIZE
    n_used = DEC_BATCH * n_pages
    n_phys = n_used + max(n_used // 4, 1)

    def w(k, shape, fan_in):
        return jax.random.normal(k, shape, jnp.float32) * (fan_in ** -0.5)

    def gain(k, shape):
        return 1.0 + 0.05 * jax.random.normal(k, shape, jnp.float32)

    page_table = jax.random.permutation(ks[5], n_phys)[:n_used].reshape(DEC_BATCH, n_pages).astype(jnp.int32)
    return {
        "x_prompt": jax.random.normal(ks[0], (BATCH, SEQ, D_MODEL), jnp.float32),
        "x_sample": jax.random.normal(ks[1], (DEC_BATCH, DEC_SEQ, D_MODEL), jnp.float32),
        "cache_k": jax.random.normal(ks[2], (N_SB_LAYERS, n_phys, PAGE_SIZE, N_HEADS, HEAD_DIM), jnp.float32),
        "cache_v": jax.random.normal(ks[3], (N_SB_LAYERS, n_phys, PAGE_SIZE, N_HEADS, HEAD_DIM), jnp.float32),
        "state_conv": jax.random.normal(ks[4], (N_CONV_LAYERS, DEC_BATCH, CONV_WIDTH - 1, D_MODEL), jnp.float32),
        "page_table": page_table,
        "meta_tokens": jax.random.normal(ks[6], (N_META, D_MODEL), jnp.float32),
        "norm_mix": gain(ks[7], (DEPTH, D_MODEL)),
        "norm_ffn": gain(ks[8], (DEPTH, D_MODEL)),
        "norm_final": gain(ks[9], (D_MODEL,)),
        "conv_w_in": w(ks[10], (N_CONV_LAYERS, D_MODEL, 3 * D_MODEL), D_MODEL),
        "conv_w": w(ks[11], (N_CONV_LAYERS, CONV_WIDTH, D_MODEL), CONV_WIDTH),
        "conv_w_out": w(ks[12], (N_CONV_LAYERS, D_MODEL, D_MODEL), D_MODEL),
        "sb_w_qkv": w(ks[13], (N_SB_LAYERS, D_MODEL, 3 * D_MODEL), D_MODEL),
        "sb_w_o": w(ks[14], (N_SB_LAYERS, D_MODEL, D_MODEL), D_MODEL),
        "sb_bias": SB_BIAS_INIT + 0.5 * jax.random.normal(ks[22], (N_SB_LAYERS, N_HEADS), jnp.float32),
        "ffn_w_gate": w(ks[15], (N_DENSE_LAYERS, D_MODEL, D_FF), D_MODEL),
        "ffn_w_up": w(ks[16], (N_DENSE_LAYERS, D_MODEL, D_FF), D_MODEL),
        "ffn_w_down": w(ks[17], (N_DENSE_LAYERS, D_FF, D_MODEL), D_FF),
        "moe_router": w(ks[18], (N_MOE_LAYERS, D_MODEL, N_EXPERTS), D_MODEL),
        "moe_w_gate": w(ks[19], (N_MOE_LAYERS, N_EXPERTS, D_MODEL, D_FF), D_MODEL),
        "moe_w_up": w(ks[20], (N_MOE_LAYERS, N_EXPERTS, D_MODEL, D_FF), D_MODEL),
        "moe_w_down": w(ks[21], (N_MOE_LAYERS, N_EXPERTS, D_FF, D_MODEL), D_FF),
    }


def reference(x_prompt, x_sample, cache_k, cache_v, state_conv, page_table, meta_tokens,
              norm_mix, norm_ffn, norm_final, conv_w_in, conv_w, conv_w_out, sb_w_qkv, sb_w_o, sb_bias,
              ffn_w_gate, ffn_w_up, ffn_w_down, moe_router, moe_w_gate, moe_w_up, moe_w_down):
    n_p = x_prompt.shape[0]
    meta = jnp.broadcast_to(meta_tokens[None].astype(x_prompt.dtype), (n_p, N_META, D_MODEL))
    xp = jnp.concatenate([meta, x_prompt], axis=1)
    xs = x_sample
    k_p, v_p, k_s, v_s, conv_p, conv_s = [], [], [], [], [], []
    for i in range(DEPTH):
        j = i // N_MIXERS
        hp = rms_norm(xp, norm_mix[i])
        hs = rms_norm(xs, norm_mix[i])
        if i % N_MIXERS == 0:
            zero_prev = jnp.zeros((n_p, CONV_WIDTH - 1, D_MODEL), xp.dtype)
            yp, cp = short_conv_mixer(hp, zero_prev, conv_w_in[j], conv_w[j], conv_w_out[j])
            ys, cs = short_conv_mixer(hs, state_conv[j], conv_w_in[j], conv_w[j], conv_w_out[j])
            conv_p.append(cp)
            conv_s.append(cs)
        else:
            yp, kp_new, vp_new = sb_mixer_prompt(hp, sb_w_qkv[j], sb_w_o[j], sb_bias[j])
            ys, ks_new, vs_new = sb_mixer_sample(hs, cache_k, cache_v, j, page_table, sb_w_qkv[j], sb_w_o[j], sb_bias[j])
            k_p.append(kp_new)
            v_p.append(vp_new)
            k_s.append(ks_new)
            v_s.append(vs_new)
        xp = xp + yp
        xs = xs + ys
        hp = rms_norm(xp, norm_ffn[i])
        hs = rms_norm(xs, norm_ffn[i])
        f = i // 2
        if i % 2 == 0:
            xp = xp + swiglu(hp, ffn_w_gate[f], ffn_w_up[f], ffn_w_down[f])
            xs = xs + swiglu(hs, ffn_w_gate[f], ffn_w_up[f], ffn_w_down[f])
        else:
            xp = xp + moe_ffn(hp, moe_router[f], moe_w_gate[f], moe_w_up[f], moe_w_down[f])
            xs = xs + moe_ffn(hs, moe_router[f], moe_w_gate[f], moe_w_up[f], moe_w_down[f])
    y_prompt = rms_norm(xp, norm_final)[:, N_META:]
    y_sample = rms_norm(xs, norm_final)
    return (y_prompt, y_sample, jnp.stack(k_p), jnp.stack(v_p), jnp.stack(k_s), jnp.stack(v_s),
            jnp.stack(conv_p), jnp.stack(conv_s))
```

```python
import functools

import jax
import jax.numpy as jnp
from jax import lax
from jax.experimental import pallas as pl
from jax.experimental.pallas import tpu as pltpu

F32 = jnp.float32
BF16 = jnp.bfloat16
I32 = jnp.int32

EPS = 1e-6
N_HEADS = 16
N_META = 16
TOP_K = 2
CONV_WIDTH = 3

V7X_SUBLANES = 8
V7X_LANES = 128
V7X_VMEM_BYTES = 64 * 1024 * 1024
VMEM_LIMIT = 56 * 1024 * 1024

ROW_TILE = 912
CONV_TILE = 432
FF_TILE = 512
ATT_TILE = 256
EXP_TILE = 512
GATHER_TILE = 304


def _params(sem):
    return pltpu.CompilerParams(dimension_semantics=sem, vmem_limit_bytes=VMEM_LIMIT)


def _row_tile(rows, target):
    best = None
    for t in range(16, min(rows, target) + 1, 16):
        if rows % t == 0:
            best = t
    assert best is not None, rows
    return best


def _lane_tile(cols, target):
    best = None
    for t in range(V7X_LANES, min(cols, target) + 1, V7X_LANES):
        if cols % t == 0:
            best = t
    assert best is not None, cols
    return best


def _rms(x, g):
    y = x * lax.rsqrt(jnp.mean(x * x, axis=-1, keepdims=True) + EPS)
    return y * g


def _softplus(z):
    return jnp.maximum(z, 0.0) + jnp.log(1.0 + jnp.exp(-jnp.abs(z)))


def _mm(a, b):
    return jnp.dot(a, b, preferred_element_type=F32)


def _conv_prompt_kernel(x_ref, g_ref, win_ref, cw_ref, wout_ref, o_ref, st_ref, ubuf,
                        *, tiles_per_seq):
    i = pl.program_id(0)
    tm, d = x_ref.shape

    @pl.when(i % tiles_per_seq == 0)
    def _():
        ubuf[0:8, :] = jnp.zeros((8, d), F32)

    x = x_ref[...]
    hb = _rms(x, g_ref[...]).astype(BF16)
    b_gate = _mm(hb, win_ref[:, 0:d])
    u = _mm(hb, win_ref[:, d:2 * d]) * _mm(hb, win_ref[:, 2 * d:3 * d])
    ubuf[8:8 + tm, :] = u
    conv = ubuf[6:6 + tm, :] * cw_ref[0:1, :]
    conv = conv + ubuf[7:7 + tm, :] * cw_ref[1:2, :]
    conv = conv + u * cw_ref[2:3, :]
    y = _mm((b_gate * conv).astype(BF16), wout_ref[...])
    o_ref[...] = x + y
    last = ubuf[tm:tm + 8, :]
    ubuf[0:8, :] = last
    st_ref[0] = last


def _conv_mixer_prompt(x, g, w_in, cw, w_out, seq_len):
    rows, d = x.shape
    tm = _row_tile(seq_len, CONV_TILE)
    tiles_per_seq = seq_len // tm
    n_seq = rows // seq_len
    out, st = pl.pallas_call(
        functools.partial(_conv_prompt_kernel, tiles_per_seq=tiles_per_seq),
        grid=(rows // tm,),
        in_specs=[
            pl.BlockSpec((tm, d), lambda i: (i, 0)),
            pl.BlockSpec((1, d), lambda i: (0, 0)),
            pl.BlockSpec((d, 3 * d), lambda i: (0, 0)),
            pl.BlockSpec((CONV_WIDTH, d), lambda i: (0, 0)),
            pl.BlockSpec((d, d), lambda i: (0, 0)),
        ],
        out_specs=[
            pl.BlockSpec((tm, d), lambda i: (i, 0)),
            pl.BlockSpec((1, 8, d), lambda i: (i // tiles_per_seq, 0, 0)),
        ],
        out_shape=[jax.ShapeDtypeStruct((rows, d), F32),
                   jax.ShapeDtypeStruct((n_seq, 8, d), F32)],
        scratch_shapes=[pltpu.VMEM((tm + 8, d), F32)],
        compiler_params=_params(("arbitrary",)),
        name="conv_mixer_prompt",
    )(x, g, w_in, cw, w_out)
    return out, st[:, 8 - (CONV_WIDTH - 1):, :]


def _conv_sample_kernel(x_ref, s0_ref, s1_ref, g_ref, win_ref, cw_ref, wout_ref, o_ref, u_ref):
    d = x_ref.shape[1]
    x = x_ref[...]
    hb = _rms(x, g_ref[...]).astype(BF16)
    b_gate = _mm(hb, win_ref[:, 0:d])
    u = _mm(hb, win_ref[:, d:2 * d]) * _mm(hb, win_ref[:, 2 * d:3 * d])
    conv = s0_ref[...] * cw_ref[0:1, :]
    conv = conv + s1_ref[...] * cw_ref[1:2, :]
    conv = conv + u * cw_ref[2:3, :]
    y = _mm((b_gate * conv).astype(BF16), wout_ref[...])
    o_ref[...] = x + y
    u_ref[...] = u


def _conv_mixer_sample(x, state, g, w_in, cw, w_out):
    n, d = x.shape
    full = lambda shape: pl.BlockSpec(shape, lambda i: (0,) * len(shape))
    out, u = pl.pallas_call(
        _conv_sample_kernel,
        grid=(1,),
        in_specs=[full((n, d)), full((n, d)), full((n, d)), full((1, d)),
                  full((d, 3 * d)), full((CONV_WIDTH, d)), full((d, d))],
        out_specs=[full((n, d)), full((n, d))],
        out_shape=[jax.ShapeDtypeStruct((n, d), F32)] * 2,
        compiler_params=_params(("arbitrary",)),
        name="conv_mixer_sample",
    )(x, state[:, 0], state[:, 1], g, w_in, cw, w_out)
    return out, jnp.stack([state[:, 1], u], axis=1)


def _ffn_kernel(x_ref, g_ref, wg_ref, wu_ref, wd_ref, o_ref, hb, acc):
    f = pl.program_id(1)

    @pl.when(f == 0)
    def _():
        x = x_ref[...]
        hb[...] = _rms(x, g_ref[...]).astype(BF16)
        acc[...] = x

    h = hb[...]
    gate = _mm(h, wg_ref[...])
    act = (gate * jax.nn.sigmoid(gate)) * _mm(h, wu_ref[...])
    acc[...] += _mm(act.astype(BF16), wd_ref[...])

    @pl.when(f == pl.num_programs(1) - 1)
    def _():
        o_ref[...] = acc[...]


def _ffn(x, g, wg, wu, wd):
    rows, d = x.shape
    dff = wg.shape[1]
    tm = _row_tile(rows, ROW_TILE)
    tf = _lane_tile(dff, FF_TILE)
    return pl.pallas_call(
        _ffn_kernel,
        grid=(rows // tm, dff // tf),
        in_specs=[
            pl.BlockSpec((tm, d), lambda i, f: (i, 0)),
            pl.BlockSpec((1, d), lambda i, f: (0, 0)),
            pl.BlockSpec((d, tf), lambda i, f: (0, f)),
            pl.BlockSpec((d, tf), lambda i, f: (0, f)),
            pl.BlockSpec((tf, d), lambda i, f: (f, 0)),
        ],
        out_specs=pl.BlockSpec((tm, d), lambda i, f: (i, 0)),
        out_shape=jax.ShapeDtypeStruct((rows, d), F32),
        scratch_shapes=[pltpu.VMEM((tm, d), BF16), pltpu.VMEM((tm, d), F32)],
        compiler_params=_params(("arbitrary", "arbitrary")),
        name="ffn_dense",
    )(x, g, wg, wu, wd)


def _qkv_kernel(x_ref, g_ref, w_ref, q_ref, k_ref, v_ref, *, q_scale):
    d = x_ref.shape[1]
    hb = _rms(x_ref[...], g_ref[...]).astype(BF16)
    q_ref[...] = (_mm(hb, w_ref[:, 0:d]) * q_scale).astype(BF16)
    k_ref[...] = _mm(hb, w_ref[:, d:2 * d])
    v_ref[...] = _mm(hb, w_ref[:, 2 * d:3 * d])


def _qkv(x, g, w):
    rows, d = x.shape
    tm = _row_tile(rows, ROW_TILE)
    q_scale = float((d // N_HEADS) ** -0.5)
    row = pl.BlockSpec((tm, d), lambda i: (i, 0))
    return pl.pallas_call(
        functools.partial(_qkv_kernel, q_scale=q_scale),
        grid=(rows // tm,),
        in_specs=[row, pl.BlockSpec((1, d), lambda i: (0, 0)),
                  pl.BlockSpec((d, 3 * d), lambda i: (0, 0))],
        out_specs=[row, row, row],
        out_shape=[jax.ShapeDtypeStruct((rows, d), BF16),
                   jax.ShapeDtypeStruct((rows, d), F32),
                   jax.ShapeDtypeStruct((rows, d), F32)],
        compiler_params=_params(("arbitrary",)),
        name="qkv_proj",
    )(x, g, w)


def _proj_residual_kernel(o_ref, w_ref, x_ref, y_ref):
    y_ref[...] = x_ref[...] + _mm(o_ref[...], w_ref[...])


def _proj_residual(o, w, x):
    rows, d = x.shape
    tm = _row_tile(rows, ROW_TILE)
    row = pl.BlockSpec((tm, d), lambda i: (i, 0))
    return pl.pallas_call(
        _proj_residual_kernel,
        grid=(rows // tm,),
        in_specs=[row, pl.BlockSpec((d, d), lambda i: (0, 0)), row],
        out_specs=row,
        out_shape=jax.ShapeDtypeStruct((rows, d), F32),
        compiler_params=_params(("arbitrary",)),
        name="attn_out_proj",
    )(o, w, x)


def _sb_prompt_kernel(bias_ref, q_ref, k_ref, vt_ref, o_ref, *, tile):
    h = pl.program_id(1)
    qi = pl.program_id(2)
    m = tile // V7X_SUBLANES
    q = q_ref[0, 0]
    bias = bias_ref[h]
    hd = q.shape[1]
    lane = lax.broadcasted_iota(I32, (V7X_SUBLANES, tile), 1)
    sub = lax.broadcasted_iota(I32, (V7X_SUBLANES, tile), 0)
    diag = lane - m * sub

    def block(kb, carry, acc, masked):
        start = pl.multiple_of(kb * tile, tile)
        k = k_ref[0, 0, pl.ds(start, tile), :]
        s = lax.dot_general(k, q, (((1,), (1,)), ((), ())), preferred_element_type=F32)
        z = s + bias
        sp = _softplus(z)
        lb = z - sp
        sps = []
        for r in range(m):
            spr = sp[8 * r:8 * r + 8]
            if masked:
                spr = jnp.where(diag > r, spr, 0.0)
            sps.append(spr)
        tot = sps[0]
        for r in range(1, m):
            tot = tot + sps[r]
        off = carry
        for j in range(1, V7X_SUBLANES):
            rowj = jnp.broadcast_to(tot[j:j + 1], tot.shape)
            off = off + jnp.where(sub < j, rowj, 0.0)
        run = off
        outs = [None] * m
        for r in reversed(range(m)):
            a = jnp.exp(lb[8 * r:8 * r + 8] - run)
            if masked:
                a = jnp.where(diag > r, a, 0.0)
            outs[r] = a
            run = run + sps[r]
        a_full = jnp.concatenate(outs, axis=0).astype(BF16)
        vt = vt_ref[0, 0, :, pl.ds(start, tile)]
        acc = acc + _mm(vt, a_full)
        carry = jnp.broadcast_to(off[0:1] + tot[0:1], tot.shape)
        return carry, acc

    carry0 = jnp.zeros((V7X_SUBLANES, tile), F32)
    acc0 = jnp.zeros((hd, tile), F32)
    carry, acc = block(qi, carry0, acc0, True)

    def body(j, c):
        return block(qi - 1 - j, c[0], c[1], False)

    carry, acc = lax.fori_loop(0, qi, body, (carry, acc))
    o_ref[0, 0] = acc.astype(o_ref.dtype)


def _sb_attention_prompt(q, k, v, bias, n_seq, seq_len):
    rows, d = q.shape
    hd = d // N_HEADS
    tile = ATT_TILE
    m = tile // V7X_SUBLANES
    n_blk = -(-seq_len // tile)
    t_pad = n_blk * tile
    pad = ((0, 0), (0, t_pad - seq_len), (0, 0), (0, 0))

    def heads(a):
        return jnp.pad(a.reshape(n_seq, seq_len, N_HEADS, hd), pad)

    def permute_keys(a):
        a = a.reshape(n_seq, n_blk, V7X_SUBLANES, m, N_HEADS, hd)
        return a.transpose(0, 1, 3, 2, 4, 5).reshape(n_seq, t_pad, N_HEADS, hd)

    qh = heads(q).transpose(0, 2, 1, 3)
    kp = permute_keys(heads(k.astype(BF16))).transpose(0, 2, 1, 3)
    vt = permute_keys(heads(v.astype(BF16))).transpose(0, 2, 3, 1)

    out = pl.pallas_call(
        functools.partial(_sb_prompt_kernel, tile=tile),
        grid=(n_seq, N_HEADS, n_blk),
        in_specs=[
            pl.BlockSpec(memory_space=pltpu.SMEM),
            pl.BlockSpec((1, 1, tile, hd), lambda b, h, i: (b, h, i, 0)),
            pl.BlockSpec((1, 1, t_pad, hd), lambda b, h, i: (b, h, 0, 0)),
            pl.BlockSpec((1, 1, hd, t_pad), lambda b, h, i: (b, h, 0, 0)),
        ],
        out_specs=pl.BlockSpec((1, 1, hd, tile), lambda b, h, i: (b, h, 0, i)),
        out_shape=jax.ShapeDtypeStruct((n_seq, N_HEADS, hd, t_pad), BF16),
        compiler_params=_params(("arbitrary", "arbitrary", "arbitrary")),
        name="sb_attention_prompt",
    )(bias, qh, kp, vt)
    o = out.transpose(0, 3, 1, 2)[:, :seq_len]
    return o.reshape(rows, d)


def _sb_decode_kernel(pt_ref, bias_ref, q_ref, e_ref, et_ref, tri_ref, k_ref, v_ref, o_ref,
                      carry, acc):
    del pt_ref
    p = pl.program_id(1)
    ps, d = k_ref.shape[2], k_ref.shape[3]

    @pl.when(p == 0)
    def _():
        carry[...] = jnp.zeros_like(carry)
        acc[...] = jnp.zeros_like(acc)

    prod = (k_ref[0, 0] * q_ref[0]).astype(BF16)
    z = _mm(prod, e_ref[...]) + bias_ref[...]
    sp = _softplus(z)
    lb = z - sp
    hi = sp.astype(BF16)
    r1 = sp - hi.astype(F32)
    mid = r1.astype(BF16)
    lo = (r1 - mid.astype(F32)).astype(BF16)
    tri = tri_ref[...]
    after = _mm(tri, hi) + _mm(tri, mid) + _mm(tri, lo)
    c = carry[0:1, :]
    a = jnp.exp(lb - (after + c))
    carry[...] = jnp.broadcast_to(c + after[0:1, :] + sp[0:1, :], carry.shape)
    a_wide = _mm(a.astype(BF16), et_ref[...])
    contrib = v_ref[0, 0] * a_wide
    part = contrib[0:8]
    for r in range(1, ps // 8):
        part = part + contrib[8 * r:8 * r + 8]
    acc[...] += part

    @pl.when(p == pl.num_programs(1) - 1)
    def _():
        o_ref[0] = jnp.sum(acc[...], axis=0, keepdims=True).astype(o_ref.dtype)


def _sb_attention_sample(q, cache_k, cache_v, layer, page_table, bias):
    n, d = q.shape
    hd = d // N_HEADS
    n_pages = page_table.shape[1]
    n_phys, ps = cache_k.shape[1], cache_k.shape[2]
    ck = cache_k.reshape(cache_k.shape[0], n_phys, ps, d)
    cv = cache_v.reshape(cache_v.shape[0], n_phys, ps, d)
    lanes = V7X_LANES
    head_of = jnp.arange(d, dtype=I32) // hd
    e = (head_of[:, None] == jnp.arange(lanes, dtype=I32)[None, :]).astype(BF16)
    et = e.T
    tri = (jnp.arange(ps)[None, :] > jnp.arange(ps)[:, None]).astype(BF16)
    bias_row = jnp.zeros((1, lanes), F32).at[0, :N_HEADS].set(bias)
    q3 = q.astype(F32).reshape(n, 1, d)

    page = lambda i, p, pt: (layer, pt[i, n_pages - 1 - p], 0, 0)
    const2 = lambda i, p, pt: (0, 0)
    out = pl.pallas_call(
        _sb_decode_kernel,
        grid_spec=pltpu.PrefetchScalarGridSpec(
            num_scalar_prefetch=1,
            grid=(n, n_pages),
            in_specs=[
                pl.BlockSpec((1, lanes), const2),
                pl.BlockSpec((1, 1, d), lambda i, p, pt: (i, 0, 0)),
                pl.BlockSpec((d, lanes), const2),
                pl.BlockSpec((lanes, d), const2),
                pl.BlockSpec((ps, ps), const2),
                pl.BlockSpec((1, 1, ps, d), page),
                pl.BlockSpec((1, 1, ps, d), page),
            ],
            out_specs=pl.BlockSpec((1, 1, d), lambda i, p, pt: (i, 0, 0)),
            scratch_shapes=[pltpu.VMEM((8, lanes), F32), pltpu.VMEM((8, d), F32)],
        ),
        out_shape=jax.ShapeDtypeStruct((n, 1, d), BF16),
        compiler_params=_params(("arbitrary", "arbitrary")),
        name="sb_attention_decode",
    )(page_table, bias_row, q3, e, et, tri, ck, cv)
    return out.reshape(n, d)


def _router_kernel(x_ref, g_ref, rw_ref, tri_ref, cnt0_ref, h_ref, info_ref, cnt_ref, run,
                   *, n_exp):
    i = pl.program_id(0)
    tm = x_ref.shape[0]
    lanes = rw_ref.shape[1]

    @pl.when(i == 0)
    def _():
        run[...] = cnt0_ref[...]

    h = _rms(x_ref[...], g_ref[...])
    h_ref[...] = h
    logits = jnp.dot(h, rw_ref[...], preferred_element_type=F32,
                     precision=lax.Precision.HIGHEST)
    lane = lax.broadcasted_iota(I32, (tm, lanes), 1)
    valid = lane < n_exp
    lg = jnp.where(valid, logits, -jnp.inf)
    ex = jnp.exp(lg - jnp.max(lg, axis=1, keepdims=True))
    probs = ex / jnp.sum(ex, axis=1, keepdims=True)
    p1 = jnp.where(valid, probs, -1.0)
    m1 = jnp.max(p1, axis=1, keepdims=True)
    i1 = jnp.min(jnp.where(p1 == m1, lane, lanes), axis=1, keepdims=True)
    p2 = jnp.where(lane == i1, -1.0, p1)
    m2 = jnp.max(p2, axis=1, keepdims=True)
    i2 = jnp.min(jnp.where(p2 == m2, lane, lanes), axis=1, keepdims=True)
    den = m1 + m2
    g1 = m1 / den
    g2 = m2 / den
    hot = ((lane == i1) | (lane == i2)).astype(BF16)
    before = _mm(tri_ref[...], hot) + run[0:1, :]
    r1 = jnp.sum(jnp.where(lane == i1, before, 0.0), axis=1, keepdims=True)
    r2 = jnp.sum(jnp.where(lane == i2, before, 0.0), axis=1, keepdims=True)
    info = jnp.where(lane == 0, i1.astype(F32), 0.0)
    info = jnp.where(lane == 1, i2.astype(F32), info)
    info = jnp.where(lane == 2, g1, info)
    info = jnp.where(lane == 3, g2, info)
    info = jnp.where(lane == 4, r1, info)
    info = jnp.where(lane == 5, r2, info)
    info_ref[...] = info
    run[...] += _mm(jnp.ones((8, tm), BF16), hot)
    cnt_ref[...] = run[...]


def _router(x, g, rw, cnt0):
    rows, d = x.shape
    n_exp = rw.shape[1]
    lanes = V7X_LANES
    tm = _row_tile(rows, ROW_TILE)
    rw_pad = jnp.zeros((d, lanes), F32).at[:, :n_exp].set(rw)
    tri = (jnp.arange(tm)[None, :] < jnp.arange(tm)[:, None]).astype(BF16)
    row = pl.BlockSpec((tm, d), lambda i: (i, 0))
    const = lambda shape: pl.BlockSpec(shape, lambda i: (0, 0))
    return pl.pallas_call(
        functools.partial(_router_kernel, n_exp=n_exp),
        grid=(rows // tm,),
        in_specs=[row, const((1, d)), const((d, lanes)), const((tm, tm)), const((8, lanes))],
        out_specs=[row, pl.BlockSpec((tm, lanes), lambda i: (i, 0)), const((8, lanes))],
        out_shape=[jax.ShapeDtypeStruct((rows, d), F32),
                   jax.ShapeDtypeStruct((rows, lanes), F32),
                   jax.ShapeDtypeStruct((8, lanes), F32)],
        scratch_shapes=[pltpu.VMEM((8, lanes), F32)],
        compiler_params=_params(("arbitrary",)),
        name="moe_router",
    )(x, g, rw_pad, tri, cnt0)


def _row_copy(src, dst, sem):
    return pltpu.make_async_copy(src, dst, sem)


def _dispatch_kernel(dest_ref, h_ref, xs_in, xs_out, sem):
    del xs_in
    tg = h_ref.shape[0]

    def start(r, _):
        for k in range(TOP_K):
            dst = dest_ref[0, 0, TOP_K * r + k]
            _row_copy(h_ref.at[pl.ds(r, 1)], xs_out.at[pl.ds(dst, 1)], sem).start()
        return 0

    lax.fori_loop(0, tg, start, 0)

    def wait(r, _):
        for k in range(TOP_K):
            _row_copy(h_ref.at[pl.ds(0, 1)], xs_out.at[pl.ds(0, 1)], sem).wait()
        return 0

    lax.fori_loop(0, tg, wait, 0)


def _dispatch(h, dest, xs):
    rows, d = h.shape
    tg = _row_tile(rows, GATHER_TILE)
    dest3 = dest.reshape(rows // tg, 1, TOP_K * tg)
    return pl.pallas_call(
        _dispatch_kernel,
        grid=(rows // tg,),
        in_specs=[
            pl.BlockSpec((1, 1, TOP_K * tg), lambda i: (i, 0, 0), memory_space=pltpu.SMEM),
            pl.BlockSpec((tg, d), lambda i: (i, 0)),
            pl.BlockSpec(memory_space=pl.ANY),
        ],
        out_specs=pl.BlockSpec(memory_space=pl.ANY),
        out_shape=jax.ShapeDtypeStruct(xs.shape, xs.dtype),
        scratch_shapes=[pltpu.SemaphoreType.DMA(())],
        input_output_aliases={2: 0},
        compiler_params=_params(("arbitrary",)),
        name="moe_dispatch",
    )(dest3, h, xs)


def _expert_kernel(te_ref, nu_ref, x_ref, wg_ref, wu_ref, wd_ref, y_ref, xb, acc):
    del te_ref
    j = pl.program_id(0)
    f = pl.program_id(1)

    @pl.when(j < nu_ref[0])
    def _():
        @pl.when(f == 0)
        def _():
            xb[...] = x_ref[...].astype(BF16)
            acc[...] = jnp.zeros_like(acc)

        h = xb[...]
        gate = _mm(h, wg_ref[0])
        act = (gate * jax.nn.sigmoid(gate)) * _mm(h, wu_ref[0])
        acc[...] += _mm(act.astype(BF16), wd_ref[0])

        @pl.when(f == pl.num_programs(1) - 1)
        def _():
            y_ref[...] = acc[...]

    @pl.when((j >= nu_ref[0]) & (f == pl.num_programs(1) - 1))
    def _():
        y_ref[...] = jnp.zeros_like(y_ref)


def _experts(xs, tile_expert, n_used, wg, wu, wd):
    p_pad, d = xs.shape
    dff = wg.shape[2]
    te = EXP_TILE
    tf = _lane_tile(dff, FF_TILE)
    n_f = dff // tf

    def jj(j, nu):
        return jnp.minimum(j, nu[0] - 1)

    def ff(j, f, nu):
        return jnp.where(j < nu[0], f, n_f - 1)

    return pl.pallas_call(
        _expert_kernel,
        grid_spec=pltpu.PrefetchScalarGridSpec(
            num_scalar_prefetch=2,
            grid=(p_pad // te, n_f),
            in_specs=[
                pl.BlockSpec((te, d), lambda j, f, t, nu: (jj(j, nu), 0)),
                pl.BlockSpec((1, d, tf), lambda j, f, t, nu: (t[jj(j, nu)], 0, ff(j, f, nu))),
                pl.BlockSpec((1, d, tf), lambda j, f, t, nu: (t[jj(j, nu)], 0, ff(j, f, nu))),
                pl.BlockSpec((1, tf, d), lambda j, f, t, nu: (t[jj(j, nu)], ff(j, f, nu), 0)),
            ],
            out_specs=pl.BlockSpec((te, d), lambda j, f, t, nu: (j, 0)),
            scratch_shapes=[pltpu.VMEM((te, d), BF16), pltpu.VMEM((te, d), F32)],
        ),
        out_shape=jax.ShapeDtypeStruct((p_pad, d), F32),
        compiler_params=_params(("arbitrary", "arbitrary")),
        name="moe_experts",
    )(tile_expert, n_used, xs, wg, wu, wd)


def _combine_kernel(dest_ref, x_ref, info_ref, g_ref, ys, o_ref, buf, sem, *, final_norm):
    tg = x_ref.shape[0]

    def start(r, _):
        for k in range(TOP_K):
            src = dest_ref[0, 0, TOP_K * r + k]
            _row_copy(ys.at[pl.ds(src, 1)], buf.at[k, pl.ds(r, 1)], sem).start()
        return 0

    lax.fori_loop(0, tg, start, 0)

    def wait(r, _):
        for k in range(TOP_K):
            _row_copy(ys.at[pl.ds(0, 1)], buf.at[0, pl.ds(0, 1)], sem).wait()
        return 0

    lax.fori_loop(0, tg, wait, 0)
    info = info_ref[...]
    out = x_ref[...] + info[:, 2:3] * buf[0]
    out = out + info[:, 3:4] * buf[1]
    if final_norm:
        out = _rms(out, g_ref[...])
    o_ref[...] = out


def _combine(x, info, dest, ys, final_g):
    rows, d = x.shape
    lanes = info.shape[1]
    tg = _row_tile(rows, GATHER_TILE)
    dest3 = dest.reshape(rows // tg, 1, TOP_K * tg)
    final_norm = final_g is not None
    g = final_g if final_norm else jnp.ones((1, d), F32)
    return pl.pallas_call(
        functools.partial(_combine_kernel, final_norm=final_norm),
        grid=(rows // tg,),
        in_specs=[
            pl.BlockSpec((1, 1, TOP_K * tg), lambda i: (i, 0, 0), memory_space=pltpu.SMEM),
            pl.BlockSpec((tg, d), lambda i: (i, 0)),
            pl.BlockSpec((tg, lanes), lambda i: (i, 0)),
            pl.BlockSpec((1, d), lambda i: (0, 0)),
            pl.BlockSpec(memory_space=pl.ANY),
        ],
        out_specs=pl.BlockSpec((tg, d), lambda i: (i, 0)),
        out_shape=jax.ShapeDtypeStruct((rows, d), F32),
        scratch_shapes=[pltpu.VMEM((TOP_K, tg, d), F32), pltpu.SemaphoreType.DMA(())],
        compiler_params=_params(("arbitrary",)),
        name="moe_combine",
    )(dest3, x, info, g, ys)


def _moe(xs_list, g, rw, wg, wu, wd, final_g):
    n_exp = rw.shape[1]
    d = xs_list[0].shape[1]
    te = EXP_TILE
    cnt = jnp.zeros((8, V7X_LANES), F32)
    hs, infos = [], []
    for x in xs_list:
        h, info, cnt = _router(x, g, rw, cnt)
        hs.append(h)
        infos.append(info)
    pairs = TOP_K * sum(x.shape[0] for x in xs_list)
    n_tiles = -(-pairs // te) + n_exp
    counts = cnt[0, :n_exp].astype(I32)
    padded = ((counts + te - 1) // te) * te
    ends = jnp.cumsum(padded)
    offs = ends - padded
    n_used = (ends[-1] // te).astype(I32).reshape(1)
    tile_expert = jnp.searchsorted(ends, jnp.arange(n_tiles, dtype=I32) * te, side="right")
    tile_expert = jnp.minimum(tile_expert, n_exp - 1).astype(I32)
    dests = [offs[info[:, 0:TOP_K].astype(I32)] + info[:, 4:4 + TOP_K].astype(I32)
             for info in infos]
    xs = jnp.zeros((n_tiles * te, d), F32)
    for h, dest in zip(hs, dests):
        xs = _dispatch(h, dest, xs)
    ys = _experts(xs, tile_expert, n_used, wg, wu, wd)
    return [_combine(x, info, dest, ys, final_g) for x, info, dest in zip(xs_list, infos, dests)]


def _norm_kernel(x_ref, g_ref, o_ref):
    o_ref[...] = _rms(x_ref[...], g_ref[...])


def _final_norm(x, g):
    rows, d = x.shape
    tm = _row_tile(rows, ROW_TILE)
    row = pl.BlockSpec((tm, d), lambda i: (i, 0))
    return pl.pallas_call(
        _norm_kernel, grid=(rows // tm,),
        in_specs=[row, pl.BlockSpec((1, d), lambda i: (0, 0))], out_specs=row,
        out_shape=jax.ShapeDtypeStruct((rows, d), F32),
        compiler_params=_params(("arbitrary",)), name="final_norm",
    )(x, g)


def kernel(x_prompt, x_sample, cache_k, cache_v, state_conv, page_table, meta_tokens, norm_mix,
           norm_ffn, norm_final, conv_w_in, conv_w, conv_w_out, sb_w_qkv, sb_w_o, sb_bias,
           ffn_w_gate, ffn_w_up, ffn_w_down, moe_router, moe_w_gate, moe_w_up, moe_w_down):
    n_p, seq, d = x_prompt.shape
    n_s = x_sample.shape[0]
    hd = d // N_HEADS
    depth = norm_mix.shape[0]
    t = seq + N_META
    meta = jnp.broadcast_to(meta_tokens[None].astype(x_prompt.dtype), (n_p, N_META, d))
    xp = jnp.concatenate([meta, x_prompt], axis=1).reshape(n_p * t, d)
    xs = x_sample.reshape(n_s, d)
    row = lambda a: a.reshape(1, d)
    k_p, v_p, k_s, v_s, conv_p, conv_s = [], [], [], [], [], []
    for i in range(depth):
        j = i // 2
        last = i == depth - 1
        g_mix = row(norm_mix[i])
        g_ffn = row(norm_ffn[i])
        if i % 2 == 0:
            w_in = conv_w_in[j].astype(BF16)
            w_out = conv_w_out[j].astype(BF16)
            xp, cp = _conv_mixer_prompt(xp, g_mix, w_in, conv_w[j], w_out, t)
            xs, cs = _conv_mixer_sample(xs, state_conv[j], g_mix, w_in, conv_w[j], w_out)
            conv_p.append(cp)
            conv_s.append(cs)
            wg = ffn_w_gate[j].astype(BF16)
            wu = ffn_w_up[j].astype(BF16)
            wd = ffn_w_down[j].astype(BF16)
            xp = _ffn(xp, g_ffn, wg, wu, wd)
            xs = _ffn(xs, g_ffn, wg, wu, wd)
            if last:
                xp = _final_norm(xp, row(norm_final))
                xs = _final_norm(xs, row(norm_final))
        else:
            w_qkv = sb_w_qkv[j].astype(BF16)
            w_o = sb_w_o[j].astype(BF16)
            qp, kp, vp = _qkv(xp, g_mix, w_qkv)
            qs, ks, vs = _qkv(xs, g_mix, w_qkv)
            k_p.append(kp.reshape(n_p, t, N_HEADS, hd))
            v_p.append(vp.reshape(n_p, t, N_HEADS, hd))
            k_s.append(ks.reshape(n_s, 1, N_HEADS, hd))
            v_s.append(vs.reshape(n_s, 1, N_HEADS, hd))
            op = _sb_attention_prompt(qp, kp, vp, sb_bias[j], n_p, t)
            os_ = _sb_attention_sample(qs, cache_k, cache_v, j, page_table, sb_bias[j])
            xp = _proj_residual(op, w_o, xp)
            xs = _proj_residual(os_, w_o, xs)
            xp, xs = _moe([xp, xs], g_ffn, moe_router[j], moe_w_gate[j].astype(BF16),
                          moe_w_up[j].astype(BF16), moe_w_down[j].astype(BF16),
                          row(norm_final) if last else None)
    y_prompt = xp.reshape(n_p, t, d)[:, N_META:]
    y_sample = xs.reshape(n_s, 1, d)
    return (y_prompt, y_sample, jnp.stack(k_p), jnp.stack(v_p), jnp.stack(k_s), jnp.stack(v_s),
            jnp.stack(conv_p), jnp.stack(conv_s))
```

```python
import functools

import jax
import jax.numpy as jnp
from jax import lax
from jax.experimental import pallas as pl
from jax.experimental.pallas import tpu as pltpu

F32 = jnp.float32
BF16 = jnp.bfloat16
I32 = jnp.int32

EPS = 1e-6
N_HEADS = 16
N_META = 16
TOP_K = 2
CONV_WIDTH = 3

V7X_SUBLANES = 8
V7X_LANES = 128
V7X_VMEM_BYTES = 64 * 1024 * 1024
VMEM_LIMIT = 56 * 1024 * 1024

ROW_TILE = 912
CONV_TILE = 432
FF_TILE = 512
ATT_TILE = 256
ATT_HEADS = 4
ATT_BLOCKS = 2
QKV_TILE = 1024
DECODE_PAGES = 8
EXP_TILE = 1024
EXP_SUB = 512
GATHER_TILE = 304


def _params(sem):
    return pltpu.CompilerParams(dimension_semantics=sem, vmem_limit_bytes=VMEM_LIMIT)


def _row_tile(rows, target):
    best = None
    for t in range(16, min(rows, target) + 1, 16):
        if rows % t == 0:
            best = t
    assert best is not None, rows
    return best


def _lane_tile(cols, target):
    best = None
    for t in range(V7X_LANES, min(cols, target) + 1, V7X_LANES):
        if cols % t == 0:
            best = t
    assert best is not None, cols
    return best


def _rms(x, g):
    y = x * lax.rsqrt(jnp.mean(x * x, axis=-1, keepdims=True) + EPS)
    return y * g


def _softplus(z):
    return jnp.maximum(z, 0.0) + jnp.log(1.0 + jnp.exp(-jnp.abs(z)))


def _mm(a, b):
    return jnp.dot(a, b, preferred_element_type=F32)


def _conv_prompt_kernel(x_ref, g_ref, win_ref, cw_ref, wout_ref, o_ref, st_ref, ubuf,
                        *, tiles_per_seq):
    i = pl.program_id(0)
    tm, d = x_ref.shape

    @pl.when(i % tiles_per_seq == 0)
    def _():
        ubuf[0:8, :] = jnp.zeros((8, d), F32)

    x = x_ref[...]
    hb = _rms(x, g_ref[...]).astype(BF16)
    b_gate = _mm(hb, win_ref[:, 0:d])
    u = _mm(hb, win_ref[:, d:2 * d]) * _mm(hb, win_ref[:, 2 * d:3 * d])
    ubuf[8:8 + tm, :] = u
    conv = ubuf[6:6 + tm, :] * cw_ref[0:1, :]
    conv = conv + ubuf[7:7 + tm, :] * cw_ref[1:2, :]
    conv = conv + u * cw_ref[2:3, :]
    y = _mm((b_gate * conv).astype(BF16), wout_ref[...])
    o_ref[...] = x + y
    last = ubuf[tm:tm + 8, :]
    ubuf[0:8, :] = last
    st_ref[0] = last


def _conv_mixer_prompt(x, g, w_in, cw, w_out, seq_len):
    rows, d = x.shape
    tm = _row_tile(seq_len, CONV_TILE)
    tiles_per_seq = seq_len // tm
    n_seq = rows // seq_len
    out, st = pl.pallas_call(
        functools.partial(_conv_prompt_kernel, tiles_per_seq=tiles_per_seq),
        grid=(rows // tm,),
        in_specs=[
            pl.BlockSpec((tm, d), lambda i: (i, 0)),
            pl.BlockSpec((1, d), lambda i: (0, 0)),
            pl.BlockSpec((d, 3 * d), lambda i: (0, 0)),
            pl.BlockSpec((CONV_WIDTH, d), lambda i: (0, 0)),
            pl.BlockSpec((d, d), lambda i: (0, 0)),
        ],
        out_specs=[
            pl.BlockSpec((tm, d), lambda i: (i, 0)),
            pl.BlockSpec((1, 8, d), lambda i: (i // tiles_per_seq, 0, 0)),
        ],
        out_shape=[jax.ShapeDtypeStruct((rows, d), F32),
                   jax.ShapeDtypeStruct((n_seq, 8, d), F32)],
        scratch_shapes=[pltpu.VMEM((tm + 8, d), F32)],
        compiler_params=_params(("arbitrary",)),
        name="conv_mixer_prompt",
    )(x, g, w_in, cw, w_out)
    return out, st[:, 8 - (CONV_WIDTH - 1):, :]


def _conv_sample_kernel(x_ref, s0_ref, s1_ref, g_ref, win_ref, cw_ref, wout_ref, o_ref, u_ref):
    d = x_ref.shape[1]
    x = x_ref[...]
    hb = _rms(x, g_ref[...]).astype(BF16)
    b_gate = _mm(hb, win_ref[:, 0:d])
    u = _mm(hb, win_ref[:, d:2 * d]) * _mm(hb, win_ref[:, 2 * d:3 * d])
    conv = s0_ref[...] * cw_ref[0:1, :]
    conv = conv + s1_ref[...] * cw_ref[1:2, :]
    conv = conv + u * cw_ref[2:3, :]
    y = _mm((b_gate * conv).astype(BF16), wout_ref[...])
    o_ref[...] = x + y
    u_ref[...] = u


def _conv_mixer_sample(x, state, g, w_in, cw, w_out):
    n, d = x.shape
    full = lambda shape: pl.BlockSpec(shape, lambda i: (0,) * len(shape))
    out, u = pl.pallas_call(
        _conv_sample_kernel,
        grid=(1,),
        in_specs=[full((n, d)), full((n, d)), full((n, d)), full((1, d)),
                  full((d, 3 * d)), full((CONV_WIDTH, d)), full((d, d))],
        out_specs=[full((n, d)), full((n, d))],
        out_shape=[jax.ShapeDtypeStruct((n, d), F32)] * 2,
        compiler_params=_params(("arbitrary",)),
        name="conv_mixer_sample",
    )(x, state[:, 0], state[:, 1], g, w_in, cw, w_out)
    return out, jnp.stack([state[:, 1], u], axis=1)


def _ffn_kernel(x_ref, g_ref, wg_ref, wu_ref, wd_ref, o_ref, hb, acc):
    f = pl.program_id(1)

    @pl.when(f == 0)
    def _():
        x = x_ref[...]
        hb[...] = _rms(x, g_ref[...]).astype(BF16)
        acc[...] = x

    h = hb[...]
    gate = _mm(h, wg_ref[...])
    act = (gate * jax.nn.sigmoid(gate)) * _mm(h, wu_ref[...])
    acc[...] += _mm(act.astype(BF16), wd_ref[...])

    @pl.when(f == pl.num_programs(1) - 1)
    def _():
        o_ref[...] = acc[...]


def _ffn(x, g, wg, wu, wd):
    rows, d = x.shape
    dff = wg.shape[1]
    tm = _row_tile(rows, ROW_TILE)
    tf = _lane_tile(dff, FF_TILE)
    return pl.pallas_call(
        _ffn_kernel,
        grid=(rows // tm, dff // tf),
        in_specs=[
            pl.BlockSpec((tm, d), lambda i, f: (i, 0)),
            pl.BlockSpec((1, d), lambda i, f: (0, 0)),
            pl.BlockSpec((d, tf), lambda i, f: (0, f)),
            pl.BlockSpec((d, tf), lambda i, f: (0, f)),
            pl.BlockSpec((tf, d), lambda i, f: (f, 0)),
        ],
        out_specs=pl.BlockSpec((tm, d), lambda i, f: (i, 0)),
        out_shape=jax.ShapeDtypeStruct((rows, d), F32),
        scratch_shapes=[pltpu.VMEM((tm, d), BF16), pltpu.VMEM((tm, d), F32)],
        compiler_params=_params(("arbitrary", "arbitrary")),
        name="ffn_dense",
    )(x, g, wg, wu, wd)


def _qkv_kernel(x_ref, g_ref, w_ref, q_ref, k_ref, v_ref, *, q_scale):
    d = x_ref.shape[1]
    hb = _rms(x_ref[...], g_ref[...]).astype(BF16)
    q_ref[...] = (_mm(hb, w_ref[:, 0:d]) * q_scale).astype(BF16)
    k_ref[...] = _mm(hb, w_ref[:, d:2 * d])
    v_ref[...] = _mm(hb, w_ref[:, 2 * d:3 * d])


def _qkv(x, g, w):
    rows, d = x.shape
    tm = _row_tile(rows, ROW_TILE)
    q_scale = float((d // N_HEADS) ** -0.5)
    row = pl.BlockSpec((tm, d), lambda i: (i, 0))
    return pl.pallas_call(
        functools.partial(_qkv_kernel, q_scale=q_scale),
        grid=(rows // tm,),
        in_specs=[row, pl.BlockSpec((1, d), lambda i: (0, 0)),
                  pl.BlockSpec((d, 3 * d), lambda i: (0, 0))],
        out_specs=[row, row, row],
        out_shape=[jax.ShapeDtypeStruct((rows, d), BF16),
                   jax.ShapeDtypeStruct((rows, d), F32),
                   jax.ShapeDtypeStruct((rows, d), F32)],
        compiler_params=_params(("arbitrary",)),
        name="qkv_proj",
    )(x, g, w)


def _qkv_prompt_kernel(x_ref, g_ref, w_ref, q_ref, k_ref, v_ref, kp_ref, vt_ref, kbuf, vbuf,
                       *, q_scale, seq_len, blk):
    i = pl.program_id(1)
    tm, d = x_ref.shape[1], x_ref.shape[2]
    m = blk // V7X_SUBLANES
    hb = _rms(x_ref[0], g_ref[...]).astype(BF16)
    q_ref[0] = (_mm(hb, w_ref[:, 0:d]) * q_scale).astype(BF16)
    k = _mm(hb, w_ref[:, d:2 * d])
    v = _mm(hb, w_ref[:, 2 * d:3 * d])
    k_ref[0] = k
    v_ref[0] = v
    row = i * tm + lax.broadcasted_iota(I32, (tm, 1), 0)
    k = jnp.where(row < seq_len, k, 0.0)
    v = jnp.where(row < seq_len, v, 0.0)
    n_col = d // V7X_LANES
    for cc in range(n_col):
        kbuf[cc] = k[:, cc * V7X_LANES:(cc + 1) * V7X_LANES]
        vbuf[cc] = v[:, cc * V7X_LANES:(cc + 1) * V7X_LANES]

    def permuted(buf, c):
        cols = []
        for cc in range(n_col):
            slabs = [buf[cc, pl.ds(c * blk + r, V7X_SUBLANES, stride=m), :] for r in range(m)]
            cols.append(jnp.concatenate(slabs, axis=0))
        return jnp.concatenate(cols, axis=1)

    for c in range(tm // blk):
        kp_ref[0, c * blk:(c + 1) * blk, :] = permuted(kbuf, c).astype(BF16)
        vt_ref[0, :, c * blk:(c + 1) * blk] = permuted(vbuf, c).T.astype(BF16)


def _qkv_prompt(x, g, w, n_seq, seq_len):
    d = x.shape[1]
    blk = ATT_TILE
    t_pad = -(-seq_len // blk) * blk
    tm = min(QKV_TILE, t_pad)
    n_tiles = -(-seq_len // tm)
    q_scale = float((d // N_HEADS) ** -0.5)
    row = pl.BlockSpec((1, tm, d), lambda b, i: (b, i, 0))
    seq = lambda dt: jax.ShapeDtypeStruct((n_seq, seq_len, d), dt)
    return pl.pallas_call(
        functools.partial(_qkv_prompt_kernel, q_scale=q_scale, seq_len=seq_len, blk=blk),
        grid=(n_seq, n_tiles),
        in_specs=[row, pl.BlockSpec((1, d), lambda b, i: (0, 0)),
                  pl.BlockSpec((d, 3 * d), lambda b, i: (0, 0))],
        out_specs=[row, row, row, row, pl.BlockSpec((1, d, tm), lambda b, i: (b, 0, i))],
        out_shape=[seq(BF16), seq(F32), seq(F32),
                   jax.ShapeDtypeStruct((n_seq, t_pad, d), BF16),
                   jax.ShapeDtypeStruct((n_seq, d, t_pad), BF16)],
        scratch_shapes=[pltpu.VMEM((d // V7X_LANES, tm, V7X_LANES), F32)] * 2,
        compiler_params=_params(("arbitrary", "arbitrary")),
        name="qkv_proj_prompt",
    )(x.reshape(n_seq, seq_len, d), g, w)


def _proj_residual_kernel(o_ref, w_ref, x_ref, y_ref):
    y_ref[...] = x_ref[...] + _mm(o_ref[...], w_ref[...])


def _proj_residual(o, w, x):
    rows, d = x.shape
    tm = _row_tile(rows, ROW_TILE)
    row = pl.BlockSpec((tm, d), lambda i: (i, 0))
    return pl.pallas_call(
        _proj_residual_kernel,
        grid=(rows // tm,),
        in_specs=[row, pl.BlockSpec((d, d), lambda i: (0, 0)), row],
        out_specs=row,
        out_shape=jax.ShapeDtypeStruct((rows, d), F32),
        compiler_params=_params(("arbitrary",)),
        name="attn_out_proj",
    )(o, w, x)


def _tree_sum(xs):
    xs = list(xs)
    while len(xs) > 1:
        odd = [xs[-1]] if len(xs) % 2 else []
        xs = [xs[i] + xs[i + 1] for i in range(0, len(xs) - 1, 2)] + odd
    return xs[0]


def _neg_abs(z):
    bits = lax.bitcast_convert_type(z, jnp.uint32) | jnp.uint32(0x80000000)
    return lax.bitcast_convert_type(bits, F32)


def _sb_prompt_kernel(bias_ref, q_ref, k_ref, vt_ref, o_ref, carry, acc, zbuf,
                      *, tile, heads, hd):
    hg = pl.program_id(1)
    qi = pl.program_id(2)
    m = tile // V7X_SUBLANES
    width = heads * hd
    q_all = q_ref[0].astype(F32)
    q_lane = lax.broadcasted_iota(I32, (tile, width), 1)
    qs = [jnp.where((q_lane >= g * hd) & (q_lane < (g + 1) * hd), q_all, 0.0).astype(BF16)
          for g in range(heads)]
    biases = [bias_ref[hg * heads + g] for g in range(heads)]
    lane = lax.broadcasted_iota(I32, (V7X_SUBLANES, tile), 1)
    sub = lax.broadcasted_iota(I32, (V7X_SUBLANES, tile), 0)
    diag = lane - m * sub
    carry[...] = jnp.zeros_like(carry)
    acc[...] = jnp.zeros_like(acc)

    def blocks(lo, n, masked):
        start = lo * tile if isinstance(lo, int) else pl.multiple_of(lo * tile, tile)
        k_cat = k_ref[0, pl.ds(start, n * tile), :]
        for g in range(heads):
            s = lax.dot_general(k_cat, qs[g], (((1,), (1,)), ((), ())),
                                preferred_element_type=F32)
            zbuf[g, 0:n * tile, :] = s + biases[g]
        for g in range(heads):
            c = carry[g]
            out = acc[g]
            for u in reversed(range(n)):
                base = u * tile
                st = start + base
                if not isinstance(st, int):
                    st = pl.multiple_of(st, tile)
                zs = [zbuf[g, base + 8 * r:base + 8 * r + 8, :] for r in range(m)]
                sps = []
                for r in range(m):
                    z = zs[r]
                    spr = jnp.maximum(z, 0.0) + jnp.log(1.0 + jnp.exp(_neg_abs(z)))
                    if masked:
                        spr = jnp.where(diag > r, spr, 0.0)
                    sps.append(spr)
                tot = _tree_sum(sps)
                off = c
                for j in range(1, V7X_SUBLANES):
                    rowj = jnp.broadcast_to(tot[j:j + 1], tot.shape)
                    off = off + jnp.where(sub < j, rowj, 0.0)
                run = off
                outs = [None] * m
                for r in reversed(range(m)):
                    run = run + sps[r]
                    a = jnp.exp(zs[r] - run)
                    if masked:
                        a = jnp.where(diag > r, a, 0.0)
                    outs[r] = a
                a_full = jnp.concatenate(outs, axis=0).astype(BF16)
                vt = vt_ref[0, g * hd:(g + 1) * hd, pl.ds(st, tile)]
                out = out + _mm(vt, a_full)
                c = jnp.broadcast_to(off[0:1] + tot[0:1], tot.shape)
            carry[g] = c
            acc[g] = out

    blocks(qi, 1, True)
    nb = ATT_BLOCKS

    def body(j, c):
        blocks(qi - nb * (j + 1), nb, False)
        return c

    lax.fori_loop(0, qi // nb, body, 0)
    for left in range(1, nb):
        @pl.when(qi % nb == left)
        def _():
            blocks(0, left, False)
    pairs = []
    for g in range(0, heads, 2):
        both = jnp.concatenate([acc[g], acc[g + 1]], axis=0)
        pairs.append(both.T)
    o_ref[0] = jnp.concatenate(pairs, axis=1).astype(o_ref.dtype)


def _sb_attention_prompt(q, kp, vt, bias):
    n_seq, seq_len, d = q.shape
    hd = d // N_HEADS
    tile = ATT_TILE
    heads = ATT_HEADS
    width = heads * hd
    t_pad = kp.shape[1]
    out = pl.pallas_call(
        functools.partial(_sb_prompt_kernel, tile=tile, heads=heads, hd=hd),
        grid=(n_seq, N_HEADS // heads, t_pad // tile),
        in_specs=[
            pl.BlockSpec(memory_space=pltpu.SMEM),
            pl.BlockSpec((1, tile, width), lambda b, h, i: (b, i, h)),
            pl.BlockSpec((1, t_pad, width), lambda b, h, i: (b, 0, h)),
            pl.BlockSpec((1, width, t_pad), lambda b, h, i: (b, h, 0)),
        ],
        out_specs=pl.BlockSpec((1, tile, width), lambda b, h, i: (b, i, h)),
        out_shape=jax.ShapeDtypeStruct((n_seq, seq_len, d), BF16),
        scratch_shapes=[pltpu.VMEM((heads, V7X_SUBLANES, tile), F32),
                        pltpu.VMEM((heads, hd, tile), F32),
                        pltpu.VMEM((heads, ATT_BLOCKS * tile, tile), F32)],
        compiler_params=_params(("arbitrary", "arbitrary", "arbitrary")),
        name="sb_attention_prompt",
    )(bias, q, kp, vt)
    return out.reshape(n_seq * seq_len, d)


def _sb_decode_kernel(pt_ref, bias_ref, q_ref, e_ref, et_ref, tri_ref, *refs, n_pg):
    del pt_ref
    k_refs, v_refs = refs[:n_pg], refs[n_pg:2 * n_pg]
    o_ref, carry, acc = refs[2 * n_pg:]
    p = pl.program_id(1)
    ps = k_refs[0].shape[2]

    @pl.when(p == 0)
    def _():
        carry[...] = jnp.zeros_like(carry)
        acc[...] = jnp.zeros_like(acc)

    q = q_ref[0]
    tri = tri_ref[...]
    prod = jnp.concatenate([(k_refs[u][0, 0] * q).astype(BF16) for u in range(n_pg)], axis=0)
    z_all = _mm(prod, e_ref[...]) + bias_ref[...]
    sp_all = _softplus(z_all)
    hi = sp_all.astype(BF16)
    r1 = sp_all - hi.astype(F32)
    mid = r1.astype(BF16)
    lo = (r1 - mid.astype(F32)).astype(BF16)
    c = carry[0:1, :]
    logits = []
    for u in range(n_pg):
        rows = slice(u * ps, (u + 1) * ps)
        after = _mm(tri, hi[rows]) + _mm(tri, mid[rows]) + _mm(tri, lo[rows])
        logits.append((z_all[rows] - sp_all[rows]) - (after + c))
        c = c + after[0:1, :] + sp_all[u * ps:u * ps + 1, :]
    a_all = jnp.exp(jnp.concatenate(logits, axis=0)).astype(BF16)
    a_wide = _mm(a_all, et_ref[...])
    parts = []
    for u in range(n_pg):
        contrib = v_refs[u][0, 0] * a_wide[u * ps:(u + 1) * ps]
        parts.append(_tree_sum([contrib[8 * r:8 * r + 8] for r in range(ps // 8)]))
    carry[...] = jnp.broadcast_to(c, carry.shape)
    acc[...] += _tree_sum(parts)

    @pl.when(p == pl.num_programs(1) - 1)
    def _():
        o_ref[0] = jnp.sum(acc[...], axis=0, keepdims=True).astype(o_ref.dtype)


def _sb_attention_sample(q, cache_k, cache_v, layer, page_table, bias):
    n, d = q.shape
    hd = d // N_HEADS
    n_pages = page_table.shape[1]
    n_phys, ps = cache_k.shape[1], cache_k.shape[2]
    ck = cache_k.reshape(cache_k.shape[0], n_phys, ps, d)
    cv = cache_v.reshape(cache_v.shape[0], n_phys, ps, d)
    lanes = V7X_LANES
    head_of = jnp.arange(d, dtype=I32) // hd
    e = (head_of[:, None] == jnp.arange(lanes, dtype=I32)[None, :]).astype(BF16)
    et = e.T
    tri = (jnp.arange(ps)[None, :] > jnp.arange(ps)[:, None]).astype(BF16)
    bias_row = jnp.zeros((1, lanes), F32).at[0, :N_HEADS].set(bias)
    q3 = q.astype(F32).reshape(n, 1, d)

    n_pg = DECODE_PAGES if n_pages % DECODE_PAGES == 0 else 1

    def page(u):
        return pl.BlockSpec((1, 1, ps, d),
                            lambda i, p, pt: (layer, pt[i, n_pages - 1 - (p * n_pg + u)], 0, 0))

    const2 = lambda i, p, pt: (0, 0)
    out = pl.pallas_call(
        functools.partial(_sb_decode_kernel, n_pg=n_pg),
        grid_spec=pltpu.PrefetchScalarGridSpec(
            num_scalar_prefetch=1,
            grid=(n, n_pages // n_pg),
            in_specs=[
                pl.BlockSpec((1, lanes), const2),
                pl.BlockSpec((1, 1, d), lambda i, p, pt: (i, 0, 0)),
                pl.BlockSpec((d, lanes), const2),
                pl.BlockSpec((lanes, d), const2),
                pl.BlockSpec((ps, ps), const2),
            ] + [page(u) for u in range(n_pg)] * 2,
            out_specs=pl.BlockSpec((1, 1, d), lambda i, p, pt: (i, 0, 0)),
            scratch_shapes=[pltpu.VMEM((8, lanes), F32), pltpu.VMEM((8, d), F32)],
        ),
        out_shape=jax.ShapeDtypeStruct((n, 1, d), BF16),
        compiler_params=_params(("arbitrary", "arbitrary")),
        name="sb_attention_decode",
    )(page_table, bias_row, q3, e, et, tri, *([ck] * n_pg), *([cv] * n_pg))
    return out.reshape(n, d)


def _router_kernel(x_ref, g_ref, rw_ref, tri_ref, cnt0_ref, h_ref, info_ref, cnt_ref, run,
                   *, n_exp):
    i = pl.program_id(0)
    tm = x_ref.shape[0]
    lanes = rw_ref.shape[1]

    @pl.when(i == 0)
    def _():
        run[...] = cnt0_ref[...]

    h = _rms(x_ref[...], g_ref[...])
    h_ref[...] = h
    logits = jnp.dot(h, rw_ref[...], preferred_element_type=F32,
                     precision=lax.Precision.HIGHEST)
    lane = lax.broadcasted_iota(I32, (tm, lanes), 1)
    valid = lane < n_exp
    lg = jnp.where(valid, logits, -jnp.inf)
    ex = jnp.exp(lg - jnp.max(lg, axis=1, keepdims=True))
    probs = ex / jnp.sum(ex, axis=1, keepdims=True)
    p1 = jnp.where(valid, probs, -1.0)
    m1 = jnp.max(p1, axis=1, keepdims=True)
    i1 = jnp.min(jnp.where(p1 == m1, lane, lanes), axis=1, keepdims=True)
    p2 = jnp.where(lane == i1, -1.0, p1)
    m2 = jnp.max(p2, axis=1, keepdims=True)
    i2 = jnp.min(jnp.where(p2 == m2, lane, lanes), axis=1, keepdims=True)
    den = m1 + m2
    g1 = m1 / den
    g2 = m2 / den
    hot = ((lane == i1) | (lane == i2)).astype(BF16)
    before = _mm(tri_ref[...], hot) + run[0:1, :]
    r1 = jnp.sum(jnp.where(lane == i1, before, 0.0), axis=1, keepdims=True)
    r2 = jnp.sum(jnp.where(lane == i2, before, 0.0), axis=1, keepdims=True)
    info = jnp.where(lane == 0, i1.astype(F32), 0.0)
    info = jnp.where(lane == 1, i2.astype(F32), info)
    info = jnp.where(lane == 2, g1, info)
    info = jnp.where(lane == 3, g2, info)
    info = jnp.where(lane == 4, r1, info)
    info = jnp.where(lane == 5, r2, info)
    info_ref[...] = info
    run[...] += _mm(jnp.ones((8, tm), BF16), hot)
    cnt_ref[...] = run[...]


def _router(x, g, rw, cnt0):
    rows, d = x.shape
    n_exp = rw.shape[1]
    lanes = V7X_LANES
    tm = _row_tile(rows, ROW_TILE)
    rw_pad = jnp.zeros((d, lanes), F32).at[:, :n_exp].set(rw)
    tri = (jnp.arange(tm)[None, :] < jnp.arange(tm)[:, None]).astype(BF16)
    row = pl.BlockSpec((tm, d), lambda i: (i, 0))
    const = lambda shape: pl.BlockSpec(shape, lambda i: (0, 0))
    return pl.pallas_call(
        functools.partial(_router_kernel, n_exp=n_exp),
        grid=(rows // tm,),
        in_specs=[row, const((1, d)), const((d, lanes)), const((tm, tm)), const((8, lanes))],
        out_specs=[row, pl.BlockSpec((tm, lanes), lambda i: (i, 0)), const((8, lanes))],
        out_shape=[jax.ShapeDtypeStruct((rows, d), F32),
                   jax.ShapeDtypeStruct((rows, lanes), F32),
                   jax.ShapeDtypeStruct((8, lanes), F32)],
        scratch_shapes=[pltpu.VMEM((8, lanes), F32)],
        compiler_params=_params(("arbitrary",)),
        name="moe_router",
    )(x, g, rw_pad, tri, cnt0)


def _row_copy(src, dst, sem):
    return pltpu.make_async_copy(src, dst, sem)


def _dispatch_kernel(dest_ref, h_ref, xs_in, xs_out, sem):
    del xs_in
    tg = h_ref.shape[0]

    def start(r, _):
        for k in range(TOP_K):
            dst = dest_ref[0, 0, TOP_K * r + k]
            _row_copy(h_ref.at[pl.ds(r, 1)], xs_out.at[pl.ds(dst, 1)], sem).start()
        return 0

    lax.fori_loop(0, tg, start, 0)

    def wait(r, _):
        for k in range(TOP_K):
            _row_copy(h_ref.at[pl.ds(0, 1)], xs_out.at[pl.ds(0, 1)], sem).wait()
        return 0

    lax.fori_loop(0, tg, wait, 0)


def _dispatch(h, dest, xs):
    rows, d = h.shape
    tg = _row_tile(rows, GATHER_TILE)
    dest3 = dest.reshape(rows // tg, 1, TOP_K * tg)
    return pl.pallas_call(
        _dispatch_kernel,
        grid=(rows // tg,),
        in_specs=[
            pl.BlockSpec((1, 1, TOP_K * tg), lambda i: (i, 0, 0), memory_space=pltpu.SMEM),
            pl.BlockSpec((tg, d), lambda i: (i, 0)),
            pl.BlockSpec(memory_space=pl.ANY),
        ],
        out_specs=pl.BlockSpec(memory_space=pl.ANY),
        out_shape=jax.ShapeDtypeStruct(xs.shape, xs.dtype),
        scratch_shapes=[pltpu.SemaphoreType.DMA(())],
        input_output_aliases={2: 0},
        compiler_params=_params(("arbitrary",)),
        name="moe_dispatch",
    )(dest3, h, xs)


def _expert_kernel(te_ref, nu_ref, rows_ref, x_ref, wg_ref, wu_ref, wd_ref, y_ref, xb, acc,
                   *, sub):
    del te_ref
    j = pl.program_id(0)
    f = pl.program_id(1)
    te = x_ref.shape[0]

    @pl.when(j < nu_ref[0])
    def _():
        @pl.when(f == 0)
        def _():
            xb[...] = x_ref[...].astype(BF16)
            acc[...] = jnp.zeros_like(acc)

        wg = wg_ref[0].astype(BF16)
        wu = wu_ref[0].astype(BF16)
        wd = wd_ref[0].astype(BF16)
        for s in range(te // sub):
            @pl.when(rows_ref[j] > s * sub)
            def _():
                h = xb[s * sub:(s + 1) * sub, :]
                gate = _mm(h, wg)
                act = (gate * jax.nn.sigmoid(gate)) * _mm(h, wu)
                acc[s * sub:(s + 1) * sub, :] += _mm(act.astype(BF16), wd)

        @pl.when(f == pl.num_programs(1) - 1)
        def _():
            y_ref[...] = acc[...]

    @pl.when((j >= nu_ref[0]) & (f == pl.num_programs(1) - 1))
    def _():
        y_ref[...] = jnp.zeros_like(y_ref)


def _experts(xs, tile_expert, n_used, tile_rows, wg, wu, wd):
    p_pad, d = xs.shape
    dff = wg.shape[2]
    te = EXP_TILE
    tf = _lane_tile(dff, FF_TILE)
    n_f = dff // tf

    def jj(j, nu):
        return jnp.minimum(j, nu[0] - 1)

    def ff(j, f, nu):
        return jnp.where(j < nu[0], f, n_f - 1)

    return pl.pallas_call(
        functools.partial(_expert_kernel, sub=EXP_SUB),
        grid_spec=pltpu.PrefetchScalarGridSpec(
            num_scalar_prefetch=3,
            grid=(p_pad // te, n_f),
            in_specs=[
                pl.BlockSpec((te, d), lambda j, f, t, nu, nr: (jj(j, nu), 0)),
                pl.BlockSpec((1, d, tf), lambda j, f, t, nu, nr: (t[jj(j, nu)], 0, ff(j, f, nu))),
                pl.BlockSpec((1, d, tf), lambda j, f, t, nu, nr: (t[jj(j, nu)], 0, ff(j, f, nu))),
                pl.BlockSpec((1, tf, d), lambda j, f, t, nu, nr: (t[jj(j, nu)], ff(j, f, nu), 0)),
            ],
            out_specs=pl.BlockSpec((te, d), lambda j, f, t, nu, nr: (j, 0)),
            scratch_shapes=[pltpu.VMEM((te, d), BF16), pltpu.VMEM((te, d), F32)],
        ),
        out_shape=jax.ShapeDtypeStruct((p_pad, d), F32),
        compiler_params=_params(("arbitrary", "arbitrary")),
        name="moe_experts",
    )(tile_expert, n_used, tile_rows, xs, wg, wu, wd)


def _combine_kernel(dest_ref, x_ref, info_ref, g_ref, ys, o_ref, buf, sem, *, final_norm):
    tg = x_ref.shape[0]

    def start(r, _):
        for k in range(TOP_K):
            src = dest_ref[0, 0, TOP_K * r + k]
            _row_copy(ys.at[pl.ds(src, 1)], buf.at[k, pl.ds(r, 1)], sem).start()
        return 0

    lax.fori_loop(0, tg, start, 0)

    def wait(r, _):
        for k in range(TOP_K):
            _row_copy(ys.at[pl.ds(0, 1)], buf.at[0, pl.ds(0, 1)], sem).wait()
        return 0

    lax.fori_loop(0, tg, wait, 0)
    info = info_ref[...]
    out = x_ref[...] + info[:, 2:3] * buf[0]
    out = out + info[:, 3:4] * buf[1]
    if final_norm:
        out = _rms(out, g_ref[...])
    o_ref[...] = out


def _combine(x, info, dest, ys, final_g):
    rows, d = x.shape
    lanes = info.shape[1]
    tg = _row_tile(rows, GATHER_TILE)
    dest3 = dest.reshape(rows // tg, 1, TOP_K * tg)
    final_norm = final_g is not None
    g = final_g if final_norm else jnp.ones((1, d), F32)
    return pl.pallas_call(
        functools.partial(_combine_kernel, final_norm=final_norm),
        grid=(rows // tg,),
        in_specs=[
            pl.BlockSpec((1, 1, TOP_K * tg), lambda i: (i, 0, 0), memory_space=pltpu.SMEM),
            pl.BlockSpec((tg, d), lambda i: (i, 0)),
            pl.BlockSpec((tg, lanes), lambda i: (i, 0)),
            pl.BlockSpec((1, d), lambda i: (0, 0)),
            pl.BlockSpec(memory_space=pl.ANY),
        ],
        out_specs=pl.BlockSpec((tg, d), lambda i: (i, 0)),
        out_shape=jax.ShapeDtypeStruct((rows, d), F32),
        scratch_shapes=[pltpu.VMEM((TOP_K, tg, d), F32), pltpu.SemaphoreType.DMA(())],
        compiler_params=_params(("arbitrary",)),
        name="moe_combine",
    )(dest3, x, info, g, ys)


def _moe(xs_list, g, rw, wg, wu, wd, final_g):
    n_exp = rw.shape[1]
    d = xs_list[0].shape[1]
    te = EXP_TILE
    cnt = jnp.zeros((8, V7X_LANES), F32)
    hs, infos = [], []
    for x in xs_list:
        h, info, cnt = _router(x, g, rw, cnt)
        hs.append(h)
        infos.append(info)
    pairs = TOP_K * sum(x.shape[0] for x in xs_list)
    n_tiles = -(-pairs // te) + n_exp
    counts = cnt[0, :n_exp].astype(I32)
    padded = ((counts + te - 1) // te) * te
    ends = jnp.cumsum(padded)
    offs = ends - padded
    n_used = (ends[-1] // te).astype(I32).reshape(1)
    tile_expert = jnp.searchsorted(ends, jnp.arange(n_tiles, dtype=I32) * te, side="right")
    tile_expert = jnp.minimum(tile_expert, n_exp - 1).astype(I32)
    tile_start = jnp.arange(n_tiles, dtype=I32) * te
    tile_rows = jnp.clip(counts[tile_expert] - (tile_start - offs[tile_expert]), 0, te).astype(I32)
    dests = [offs[info[:, 0:TOP_K].astype(I32)] + info[:, 4:4 + TOP_K].astype(I32)
             for info in infos]
    xs = jnp.zeros((n_tiles * te, d), F32)
    for h, dest in zip(hs, dests):
        xs = _dispatch(h, dest, xs)
    ys = _experts(xs, tile_expert, n_used, tile_rows, wg, wu, wd)
    return [_combine(x, info, dest, ys, final_g) for x, info, dest in zip(xs_list, infos, dests)]


def _norm_kernel(x_ref, g_ref, o_ref):
    o_ref[...] = _rms(x_ref[...], g_ref[...])


def _final_norm(x, g):
    rows, d = x.shape
    tm = _row_tile(rows, ROW_TILE)
    row = pl.BlockSpec((tm, d), lambda i: (i, 0))
    return pl.pallas_call(
        _norm_kernel, grid=(rows // tm,),
        in_specs=[row, pl.BlockSpec((1, d), lambda i: (0, 0))], out_specs=row,
        out_shape=jax.ShapeDtypeStruct((rows, d), F32),
        compiler_params=_params(("arbitrary",)), name="final_norm",
    )(x, g)


def kernel(x_prompt, x_sample, cache_k, cache_v, state_conv, page_table, meta_tokens, norm_mix,
           norm_ffn, norm_final, conv_w_in, conv_w, conv_w_out, sb_w_qkv, sb_w_o, sb_bias,
           ffn_w_gate, ffn_w_up, ffn_w_down, moe_router, moe_w_gate, moe_w_up, moe_w_down):
    n_p, seq, d = x_prompt.shape
    n_s = x_sample.shape[0]
    hd = d // N_HEADS
    depth = norm_mix.shape[0]
    t = seq + N_META
    meta = jnp.broadcast_to(meta_tokens[None].astype(x_prompt.dtype), (n_p, N_META, d))
    xp = jnp.concatenate([meta, x_prompt], axis=1).reshape(n_p * t, d)
    xs = x_sample.reshape(n_s, d)
    row = lambda a: a.reshape(1, d)
    k_p, v_p, k_s, v_s, conv_p, conv_s = [], [], [], [], [], []
    for i in range(depth):
        j = i // 2
        last = i == depth - 1
        g_mix = row(norm_mix[i])
        g_ffn = row(norm_ffn[i])
        if i % 2 == 0:
            w_in = conv_w_in[j].astype(BF16)
            w_out = conv_w_out[j].astype(BF16)
            xp, cp = _conv_mixer_prompt(xp, g_mix, w_in, conv_w[j], w_out, t)
            xs, cs = _conv_mixer_sample(xs, state_conv[j], g_mix, w_in, conv_w[j], w_out)
            conv_p.append(cp)
            conv_s.append(cs)
            wg = ffn_w_gate[j].astype(BF16)
            wu = ffn_w_up[j].astype(BF16)
            wd = ffn_w_down[j].astype(BF16)
            xp = _ffn(xp, g_ffn, wg, wu, wd)
            xs = _ffn(xs, g_ffn, wg, wu, wd)
            if last:
                xp = _final_norm(xp, row(norm_final))
                xs = _final_norm(xs, row(norm_final))
        else:
            w_qkv = sb_w_qkv[j].astype(BF16)
            w_o = sb_w_o[j].astype(BF16)
            qp, kp, vp, k_att, v_att = _qkv_prompt(xp, g_mix, w_qkv, n_p, t)
            qs, ks, vs = _qkv(xs, g_mix, w_qkv)
            k_p.append(kp.reshape(n_p, t, N_HEADS, hd))
            v_p.append(vp.reshape(n_p, t, N_HEADS, hd))
            k_s.append(ks.reshape(n_s, 1, N_HEADS, hd))
            v_s.append(vs.reshape(n_s, 1, N_HEADS, hd))
            op = _sb_attention_prompt(qp, k_att, v_att, sb_bias[j])
            os_ = _sb_attention_sample(qs, cache_k, cache_v, j, page_table, sb_bias[j])
            xp = _proj_residual(op, w_o, xp)
            xs = _proj_residual(os_, w_o, xs)
            xp, xs = _moe([xp, xs], g_ffn, moe_router[j], moe_w_gate[j], moe_w_up[j],
                          moe_w_down[j], row(norm_final) if last else None)
    y_prompt = xp.reshape(n_p, t, d)[:, N_META:]
    y_sample = xs.reshape(n_s, 1, d)
    return (y_prompt, y_sample, jnp.stack(k_p), jnp.stack(v_p), jnp.stack(k_s), jnp.stack(v_s),
            jnp.stack(conv_p), jnp.stack(conv_s))
```

```python
import functools

import jax
import jax.numpy as jnp
from jax import lax
from jax.experimental import pallas as pl
from jax.experimental.pallas import tpu as pltpu

F32 = jnp.float32
BF16 = jnp.bfloat16
I32 = jnp.int32

EPS = 1e-6
N_HEADS = 16
N_META = 16
TOP_K = 2
CONV_WIDTH = 3

V7X_SUBLANES = 8
V7X_LANES = 128
V7X_VMEM_BYTES = 64 * 1024 * 1024
VMEM_LIMIT = 56 * 1024 * 1024

ROW_TILE = 912
CONV_TILE = 432
FF_TILE = 512
ATT_TILE = 256
ATT_HEADS = 4
ATT_BLOCKS = 4
QKV_TILE = 1024
DECODE_PAGES = 8
EXP_TILE = 1024
EXP_SUB = 512
GATHER_TILE = 304


def _params(sem):
    return pltpu.CompilerParams(dimension_semantics=sem, vmem_limit_bytes=VMEM_LIMIT)


def _row_tile(rows, target):
    best = None
    for t in range(16, min(rows, target) + 1, 16):
        if rows % t == 0:
            best = t
    assert best is not None, rows
    return best


def _lane_tile(cols, target):
    best = None
    for t in range(V7X_LANES, min(cols, target) + 1, V7X_LANES):
        if cols % t == 0:
            best = t
    assert best is not None, cols
    return best


def _rms(x, g):
    y = x * lax.rsqrt(jnp.mean(x * x, axis=-1, keepdims=True) + EPS)
    return y * g


def _softplus(z):
    return jnp.maximum(z, 0.0) + jnp.log(1.0 + jnp.exp(-jnp.abs(z)))


def _mm(a, b):
    return jnp.dot(a, b, preferred_element_type=F32)


def _conv_prompt_kernel(x_ref, g_ref, win_ref, cw_ref, wout_ref, o_ref, st_ref, ubuf,
                        *, tiles_per_seq):
    i = pl.program_id(0)
    tm, d = x_ref.shape

    @pl.when(i % tiles_per_seq == 0)
    def _():
        ubuf[0:8, :] = jnp.zeros((8, d), F32)

    x = x_ref[...]
    hb = _rms(x, g_ref[...]).astype(BF16)
    b_gate = _mm(hb, win_ref[:, 0:d])
    u = _mm(hb, win_ref[:, d:2 * d]) * _mm(hb, win_ref[:, 2 * d:3 * d])
    ubuf[8:8 + tm, :] = u
    conv = ubuf[6:6 + tm, :] * cw_ref[0:1, :]
    conv = conv + ubuf[7:7 + tm, :] * cw_ref[1:2, :]
    conv = conv + u * cw_ref[2:3, :]
    y = _mm((b_gate * conv).astype(BF16), wout_ref[...])
    o_ref[...] = x + y
    last = ubuf[tm:tm + 8, :]
    ubuf[0:8, :] = last
    st_ref[0] = last


def _conv_mixer_prompt(x, g, w_in, cw, w_out, seq_len):
    rows, d = x.shape
    tm = _row_tile(seq_len, CONV_TILE)
    tiles_per_seq = seq_len // tm
    n_seq = rows // seq_len
    out, st = pl.pallas_call(
        functools.partial(_conv_prompt_kernel, tiles_per_seq=tiles_per_seq),
        grid=(rows // tm,),
        in_specs=[
            pl.BlockSpec((tm, d), lambda i: (i, 0)),
            pl.BlockSpec((1, d), lambda i: (0, 0)),
            pl.BlockSpec((d, 3 * d), lambda i: (0, 0)),
            pl.BlockSpec((CONV_WIDTH, d), lambda i: (0, 0)),
            pl.BlockSpec((d, d), lambda i: (0, 0)),
        ],
        out_specs=[
            pl.BlockSpec((tm, d), lambda i: (i, 0)),
            pl.BlockSpec((1, 8, d), lambda i: (i // tiles_per_seq, 0, 0)),
        ],
        out_shape=[jax.ShapeDtypeStruct((rows, d), F32),
                   jax.ShapeDtypeStruct((n_seq, 8, d), F32)],
        scratch_shapes=[pltpu.VMEM((tm + 8, d), F32)],
        compiler_params=_params(("arbitrary",)),
        name="conv_mixer_prompt",
    )(x, g, w_in, cw, w_out)
    return out, st[:, 8 - (CONV_WIDTH - 1):, :]


def _conv_sample_kernel(x_ref, s0_ref, s1_ref, g_ref, win_ref, cw_ref, wout_ref, o_ref, u_ref):
    d = x_ref.shape[1]
    x = x_ref[...]
    hb = _rms(x, g_ref[...]).astype(BF16)
    b_gate = _mm(hb, win_ref[:, 0:d])
    u = _mm(hb, win_ref[:, d:2 * d]) * _mm(hb, win_ref[:, 2 * d:3 * d])
    conv = s0_ref[...] * cw_ref[0:1, :]
    conv = conv + s1_ref[...] * cw_ref[1:2, :]
    conv = conv + u * cw_ref[2:3, :]
    y = _mm((b_gate * conv).astype(BF16), wout_ref[...])
    o_ref[...] = x + y
    u_ref[...] = u


def _conv_mixer_sample(x, state, g, w_in, cw, w_out):
    n, d = x.shape
    full = lambda shape: pl.BlockSpec(shape, lambda i: (0,) * len(shape))
    out, u = pl.pallas_call(
        _conv_sample_kernel,
        grid=(1,),
        in_specs=[full((n, d)), full((n, d)), full((n, d)), full((1, d)),
                  full((d, 3 * d)), full((CONV_WIDTH, d)), full((d, d))],
        out_specs=[full((n, d)), full((n, d))],
        out_shape=[jax.ShapeDtypeStruct((n, d), F32)] * 2,
        compiler_params=_params(("arbitrary",)),
        name="conv_mixer_sample",
    )(x, state[:, 0], state[:, 1], g, w_in, cw, w_out)
    return out, jnp.stack([state[:, 1], u], axis=1)


def _ffn_kernel(x_ref, g_ref, wg_ref, wu_ref, wd_ref, o_ref, hb, acc):
    f = pl.program_id(1)

    @pl.when(f == 0)
    def _():
        x = x_ref[...]
        hb[...] = _rms(x, g_ref[...]).astype(BF16)
        acc[...] = x

    h = hb[...]
    gate = _mm(h, wg_ref[...])
    act = (gate * jax.nn.sigmoid(gate)) * _mm(h, wu_ref[...])
    acc[...] += _mm(act.astype(BF16), wd_ref[...])

    @pl.when(f == pl.num_programs(1) - 1)
    def _():
        o_ref[...] = acc[...]


def _ffn(x, g, wg, wu, wd):
    rows, d = x.shape
    dff = wg.shape[1]
    tm = _row_tile(rows, ROW_TILE)
    tf = _lane_tile(dff, FF_TILE)
    return pl.pallas_call(
        _ffn_kernel,
        grid=(rows // tm, dff // tf),
        in_specs=[
            pl.BlockSpec((tm, d), lambda i, f: (i, 0)),
            pl.BlockSpec((1, d), lambda i, f: (0, 0)),
            pl.BlockSpec((d, tf), lambda i, f: (0, f)),
            pl.BlockSpec((d, tf), lambda i, f: (0, f)),
            pl.BlockSpec((tf, d), lambda i, f: (f, 0)),
        ],
        out_specs=pl.BlockSpec((tm, d), lambda i, f: (i, 0)),
        out_shape=jax.ShapeDtypeStruct((rows, d), F32),
        scratch_shapes=[pltpu.VMEM((tm, d), BF16), pltpu.VMEM((tm, d), F32)],
        compiler_params=_params(("arbitrary", "arbitrary")),
        name="ffn_dense",
    )(x, g, wg, wu, wd)


def _qkv_kernel(x_ref, g_ref, w_ref, q_ref, k_ref, v_ref, *, q_scale):
    d = x_ref.shape[1]
    hb = _rms(x_ref[...], g_ref[...]).astype(BF16)
    q_ref[...] = (_mm(hb, w_ref[:, 0:d]) * q_scale).astype(BF16)
    k_ref[...] = _mm(hb, w_ref[:, d:2 * d])
    v_ref[...] = _mm(hb, w_ref[:, 2 * d:3 * d])


def _qkv(x, g, w):
    rows, d = x.shape
    tm = _row_tile(rows, ROW_TILE)
    q_scale = float((d // N_HEADS) ** -0.5)
    row = pl.BlockSpec((tm, d), lambda i: (i, 0))
    return pl.pallas_call(
        functools.partial(_qkv_kernel, q_scale=q_scale),
        grid=(rows // tm,),
        in_specs=[row, pl.BlockSpec((1, d), lambda i: (0, 0)),
                  pl.BlockSpec((d, 3 * d), lambda i: (0, 0))],
        out_specs=[row, row, row],
        out_shape=[jax.ShapeDtypeStruct((rows, d), BF16),
                   jax.ShapeDtypeStruct((rows, d), F32),
                   jax.ShapeDtypeStruct((rows, d), F32)],
        compiler_params=_params(("arbitrary",)),
        name="qkv_proj",
    )(x, g, w)


def _qkv_prompt_kernel(x_ref, g_ref, w_ref, q_ref, k_ref, v_ref, kp_ref, vt_ref, kbuf, vbuf,
                       *, q_scale, seq_len, blk):
    i = pl.program_id(1)
    tm, d = x_ref.shape[1], x_ref.shape[2]
    m = blk // V7X_SUBLANES
    hb = _rms(x_ref[0], g_ref[...]).astype(BF16)
    q_ref[0] = (_mm(hb, w_ref[:, 0:d]) * q_scale).astype(BF16)
    k = _mm(hb, w_ref[:, d:2 * d])
    v = _mm(hb, w_ref[:, 2 * d:3 * d])
    k_ref[0] = k
    v_ref[0] = v
    row = i * tm + lax.broadcasted_iota(I32, (tm, 1), 0)
    k = jnp.where(row < seq_len, k, 0.0)
    v = jnp.where(row < seq_len, v, 0.0)
    n_col = d // V7X_LANES
    for cc in range(n_col):
        kbuf[cc] = k[:, cc * V7X_LANES:(cc + 1) * V7X_LANES]
        vbuf[cc] = v[:, cc * V7X_LANES:(cc + 1) * V7X_LANES]

    def permuted(buf, c):
        cols = []
        for cc in range(n_col):
            slabs = [buf[cc, pl.ds(c * blk + r, V7X_SUBLANES, stride=m), :] for r in range(m)]
            cols.append(jnp.concatenate(slabs, axis=0))
        return jnp.concatenate(cols, axis=1)

    for c in range(tm // blk):
        kp_ref[0, c * blk:(c + 1) * blk, :] = permuted(kbuf, c).astype(BF16)
        vt_ref[0, :, c * blk:(c + 1) * blk] = permuted(vbuf, c).T.astype(BF16)


def _qkv_prompt(x, g, w, n_seq, seq_len):
    d = x.shape[1]
    blk = ATT_TILE
    t_pad = -(-seq_len // blk) * blk
    tm = min(QKV_TILE, t_pad)
    n_tiles = -(-seq_len // tm)
    q_scale = float((d // N_HEADS) ** -0.5)
    row = pl.BlockSpec((1, tm, d), lambda b, i: (b, i, 0))
    seq = lambda dt: jax.ShapeDtypeStruct((n_seq, seq_len, d), dt)
    return pl.pallas_call(
        functools.partial(_qkv_prompt_kernel, q_scale=q_scale, seq_len=seq_len, blk=blk),
        grid=(n_seq, n_tiles),
        in_specs=[row, pl.BlockSpec((1, d), lambda b, i: (0, 0)),
                  pl.BlockSpec((d, 3 * d), lambda b, i: (0, 0))],
        out_specs=[row, row, row, row, pl.BlockSpec((1, d, tm), lambda b, i: (b, 0, i))],
        out_shape=[seq(BF16), seq(F32), seq(F32),
                   jax.ShapeDtypeStruct((n_seq, t_pad, d), BF16),
                   jax.ShapeDtypeStruct((n_seq, d, t_pad), BF16)],
        scratch_shapes=[pltpu.VMEM((d // V7X_LANES, tm, V7X_LANES), F32)] * 2,
        compiler_params=_params(("arbitrary", "arbitrary")),
        name="qkv_proj_prompt",
    )(x.reshape(n_seq, seq_len, d), g, w)


def _proj_residual_kernel(o_ref, w_ref, x_ref, y_ref):
    y_ref[...] = x_ref[...] + _mm(o_ref[...], w_ref[...])


def _proj_residual(o, w, x):
    rows, d = x.shape
    tm = _row_tile(rows, ROW_TILE)
    row = pl.BlockSpec((tm, d), lambda i: (i, 0))
    return pl.pallas_call(
        _proj_residual_kernel,
        grid=(rows // tm,),
        in_specs=[row, pl.BlockSpec((d, d), lambda i: (0, 0)), row],
        out_specs=row,
        out_shape=jax.ShapeDtypeStruct((rows, d), F32),
        compiler_params=_params(("arbitrary",)),
        name="attn_out_proj",
    )(o, w, x)


def _tree_sum(xs):
    xs = list(xs)
    while len(xs) > 1:
        odd = [xs[-1]] if len(xs) % 2 else []
        xs = [xs[i] + xs[i + 1] for i in range(0, len(xs) - 1, 2)] + odd
    return xs[0]


def _neg_abs(z):
    bits = lax.bitcast_convert_type(z, jnp.uint32) | jnp.uint32(0x80000000)
    return lax.bitcast_convert_type(bits, F32)


def _sb_prompt_kernel(bias_ref, q_ref, k_ref, vt_ref, o_ref, carry, acc, zbuf,
                      *, tile, heads, hd):
    hg = pl.program_id(1)
    qi = pl.program_id(2)
    m = tile // V7X_SUBLANES
    width = heads * hd
    q_all = q_ref[0].astype(F32)
    q_lane = lax.broadcasted_iota(I32, (tile, width), 1)
    qs = [jnp.where((q_lane >= g * hd) & (q_lane < (g + 1) * hd), q_all, 0.0).astype(BF16)
          for g in range(heads)]
    biases = [bias_ref[hg * heads + g] for g in range(heads)]
    lane = lax.broadcasted_iota(I32, (V7X_SUBLANES, tile), 1)
    sub = lax.broadcasted_iota(I32, (V7X_SUBLANES, tile), 0)
    diag = lane - m * sub
    carry[...] = jnp.zeros_like(carry)
    acc[...] = jnp.zeros_like(acc)

    def blocks(lo, n, masked):
        start = lo * tile if isinstance(lo, int) else pl.multiple_of(lo * tile, tile)
        k_cat = k_ref[0, pl.ds(start, n * tile), :]
        for g in range(heads):
            s = lax.dot_general(k_cat, qs[g], (((1,), (1,)), ((), ())),
                                preferred_element_type=F32)
            zbuf[g, 0:n * tile, :] = s + biases[g]
        for g in range(heads):
            c = carry[g]
            out = acc[g]
            for u in reversed(range(n)):
                base = u * tile
                st = start + base
                if not isinstance(st, int):
                    st = pl.multiple_of(st, tile)
                zs = [zbuf[g, base + 8 * r:base + 8 * r + 8, :] for r in range(m)]
                sps = []
                for r in range(m):
                    z = zs[r]
                    spr = jnp.maximum(z, 0.0) + jnp.log(1.0 + jnp.exp(_neg_abs(z)))
                    if masked:
                        spr = jnp.where(diag > r, spr, 0.0)
                    sps.append(spr)
                tot = _tree_sum(sps)
                off = c
                for j in range(1, V7X_SUBLANES):
                    rowj = jnp.broadcast_to(tot[j:j + 1], tot.shape)
                    off = off + jnp.where(sub < j, rowj, 0.0)
                run = off
                outs = [None] * m
                for r in reversed(range(m)):
                    run = run + sps[r]
                    a = jnp.exp(zs[r] - run)
                    if masked:
                        a = jnp.where(diag > r, a, 0.0)
                    outs[r] = a
                a_full = jnp.concatenate(outs, axis=0).astype(BF16)
                vt = vt_ref[0, g * hd:(g + 1) * hd, pl.ds(st, tile)]
                out = out + _mm(vt, a_full)
                c = jnp.broadcast_to(off[0:1] + tot[0:1], tot.shape)
            carry[g] = c
            acc[g] = out

    blocks(qi, 1, True)
    nb = ATT_BLOCKS

    def body(j, c):
        blocks(qi - nb * (j + 1), nb, False)
        return c

    lax.fori_loop(0, qi // nb, body, 0)
    for left in range(1, nb):
        @pl.when(qi % nb == left)
        def _():
            blocks(0, left, False)
    pairs = []
    for g in range(0, heads, 2):
        both = jnp.concatenate([acc[g], acc[g + 1]], axis=0)
        pairs.append(both.T)
    o_ref[0] = jnp.concatenate(pairs, axis=1).astype(o_ref.dtype)


def _sb_attention_prompt(q, kp, vt, bias):
    n_seq, seq_len, d = q.shape
    hd = d // N_HEADS
    tile = ATT_TILE
    heads = ATT_HEADS
    width = heads * hd
    t_pad = kp.shape[1]
    out = pl.pallas_call(
        functools.partial(_sb_prompt_kernel, tile=tile, heads=heads, hd=hd),
        grid=(n_seq, N_HEADS // heads, t_pad // tile),
        in_specs=[
            pl.BlockSpec(memory_space=pltpu.SMEM),
            pl.BlockSpec((1, tile, width), lambda b, h, i: (b, i, h)),
            pl.BlockSpec((1, t_pad, width), lambda b, h, i: (b, 0, h)),
            pl.BlockSpec((1, width, t_pad), lambda b, h, i: (b, h, 0)),
        ],
        out_specs=pl.BlockSpec((1, tile, width), lambda b, h, i: (b, i, h)),
        out_shape=jax.ShapeDtypeStruct((n_seq, seq_len, d), BF16),
        scratch_shapes=[pltpu.VMEM((heads, V7X_SUBLANES, tile), F32),
                        pltpu.VMEM((heads, hd, tile), F32),
                        pltpu.VMEM((heads, ATT_BLOCKS * tile, tile), F32)],
        compiler_params=_params(("arbitrary", "arbitrary", "arbitrary")),
        name="sb_attention_prompt",
    )(bias, q, kp, vt)
    return out.reshape(n_seq * seq_len, d)


def _sb_decode_kernel(pt_ref, bias_ref, q_ref, tri_ref, *refs, n_pg):
    del pt_ref
    k_refs, v_refs = refs[:n_pg], refs[n_pg:2 * n_pg]
    o_ref, carry, acc = refs[2 * n_pg:]
    p = pl.program_id(1)
    nh = k_refs[0].shape[2]

    @pl.when(p == 0)
    def _():
        carry[...] = jnp.zeros_like(carry)
        acc[...] = jnp.zeros_like(acc)

    tri = tri_ref[...]
    c = carry[...]
    weights = []
    for u in range(n_pg):
        rows = [jnp.sum(k_refs[u][0, 0, h] * q_ref[0, h], axis=0, keepdims=True)
                for h in range(nh)]
        z = jnp.concatenate(rows, axis=0) + bias_ref[...]
        sp = _softplus(z)
        hi = sp.astype(BF16)
        r1 = sp - hi.astype(F32)
        mid = r1.astype(BF16)
        lo = (r1 - mid.astype(F32)).astype(BF16)
        after = _mm(hi, tri) + _mm(mid, tri) + _mm(lo, tri)
        weights.append(jnp.exp((z - sp) - (after + c)))
        c = c + jnp.broadcast_to(after[:, 0:1] + sp[:, 0:1], c.shape)
    carry[...] = c
    for h in range(nh):
        acc[h] += _tree_sum([v_refs[u][0, 0, h] * weights[u][h:h + 1, :] for u in range(n_pg)])

    @pl.when(p == pl.num_programs(1) - 1)
    def _():
        o_ref[0] = jnp.sum(acc[...], axis=2).astype(o_ref.dtype)


def _sb_attention_sample(q, cache_k, cache_v, layer, page_table, bias):
    n, d = q.shape
    nh = N_HEADS
    hd = d // nh
    n_pages = page_table.shape[1]
    ps = cache_k.shape[2]
    ck = jnp.transpose(cache_k, (0, 1, 3, 4, 2))
    cv = jnp.transpose(cache_v, (0, 1, 3, 4, 2))
    tri = (jnp.arange(ps)[:, None] > jnp.arange(ps)[None, :]).astype(BF16)
    bias_rep = jnp.broadcast_to(bias.astype(F32)[:, None], (nh, ps))
    q_rep = jnp.broadcast_to(q.astype(F32).reshape(n, nh, hd, 1), (n, nh, hd, ps))

    n_pg = DECODE_PAGES if n_pages % DECODE_PAGES == 0 else 1

    def page(u):
        return pl.BlockSpec(
            (1, 1, nh, hd, ps),
            lambda i, p, pt: (layer, pt[i, n_pages - 1 - (p * n_pg + u)], 0, 0, 0))

    const2 = lambda i, p, pt: (0, 0)
    out = pl.pallas_call(
        functools.partial(_sb_decode_kernel, n_pg=n_pg),
        grid_spec=pltpu.PrefetchScalarGridSpec(
            num_scalar_prefetch=1,
            grid=(n, n_pages // n_pg),
            in_specs=[
                pl.BlockSpec((nh, ps), const2),
                pl.BlockSpec((1, nh, hd, ps), lambda i, p, pt: (i, 0, 0, 0)),
                pl.BlockSpec((ps, ps), const2),
            ] + [page(u) for u in range(n_pg)] * 2,
            out_specs=pl.BlockSpec((1, nh, hd), lambda i, p, pt: (i, 0, 0)),
            scratch_shapes=[pltpu.VMEM((nh, ps), F32), pltpu.VMEM((nh, hd, ps), F32)],
        ),
        out_shape=jax.ShapeDtypeStruct((n, nh, hd), BF16),
        compiler_params=_params(("arbitrary", "arbitrary")),
        name="sb_attention_decode",
    )(page_table, bias_rep, q_rep, tri, *([ck] * n_pg), *([cv] * n_pg))
    return out.reshape(n, d)


def _router_kernel(x_ref, g_ref, rw_ref, tri_ref, cnt0_ref, h_ref, info_ref, cnt_ref, run,
                   *, n_exp):
    i = pl.program_id(0)
    tm = x_ref.shape[0]
    lanes = rw_ref.shape[1]

    @pl.when(i == 0)
    def _():
        run[...] = cnt0_ref[...]

    h = _rms(x_ref[...], g_ref[...])
    h_ref[...] = h
    logits = jnp.dot(h, rw_ref[...], preferred_element_type=F32,
                     precision=lax.Precision.HIGHEST)
    lane = lax.broadcasted_iota(I32, (tm, lanes), 1)
    valid = lane < n_exp
    lg = jnp.where(valid, logits, -jnp.inf)
    ex = jnp.exp(lg - jnp.max(lg, axis=1, keepdims=True))
    probs = ex / jnp.sum(ex, axis=1, keepdims=True)
    p1 = jnp.where(valid, probs, -1.0)
    m1 = jnp.max(p1, axis=1, keepdims=True)
    i1 = jnp.min(jnp.where(p1 == m1, lane, lanes), axis=1, keepdims=True)
    p2 = jnp.where(lane == i1, -1.0, p1)
    m2 = jnp.max(p2, axis=1, keepdims=True)
    i2 = jnp.min(jnp.where(p2 == m2, lane, lanes), axis=1, keepdims=True)
    den = m1 + m2
    g1 = m1 / den
    g2 = m2 / den
    hot = ((lane == i1) | (lane == i2)).astype(BF16)
    before = _mm(tri_ref[...], hot) + run[0:1, :]
    r1 = jnp.sum(jnp.where(lane == i1, before, 0.0), axis=1, keepdims=True)
    r2 = jnp.sum(jnp.where(lane == i2, before, 0.0), axis=1, keepdims=True)
    info = jnp.where(lane == 0, i1.astype(F32), 0.0)
    info = jnp.where(lane == 1, i2.astype(F32), info)
    info = jnp.where(lane == 2, g1, info)
    info = jnp.where(lane == 3, g2, info)
    info = jnp.where(lane == 4, r1, info)
    info = jnp.where(lane == 5, r2, info)
    info_ref[...] = info
    run[...] += _mm(jnp.ones((8, tm), BF16), hot)
    cnt_ref[...] = run[...]


def _router(x, g, rw, cnt0):
    rows, d = x.shape
    n_exp = rw.shape[1]
    lanes = V7X_LANES
    tm = _row_tile(rows, ROW_TILE)
    rw_pad = jnp.zeros((d, lanes), F32).at[:, :n_exp].set(rw)
    tri = (jnp.arange(tm)[None, :] < jnp.arange(tm)[:, None]).astype(BF16)
    row = pl.BlockSpec((tm, d), lambda i: (i, 0))
    const = lambda shape: pl.BlockSpec(shape, lambda i: (0, 0))
    return pl.pallas_call(
        functools.partial(_router_kernel, n_exp=n_exp),
        grid=(rows // tm,),
        in_specs=[row, const((1, d)), const((d, lanes)), const((tm, tm)), const((8, lanes))],
        out_specs=[row, pl.BlockSpec((tm, lanes), lambda i: (i, 0)), const((8, lanes))],
        out_shape=[jax.ShapeDtypeStruct((rows, d), F32),
                   jax.ShapeDtypeStruct((rows, lanes), F32),
                   jax.ShapeDtypeStruct((8, lanes), F32)],
        scratch_shapes=[pltpu.VMEM((8, lanes), F32)],
        compiler_params=_params(("arbitrary",)),
        name="moe_router",
    )(x, g, rw_pad, tri, cnt0)


def _row_copy(src, dst, sem):
    return pltpu.make_async_copy(src, dst, sem)


def _dispatch_kernel(dest_ref, h_ref, xs_in, xs_out, sem):
    del xs_in
    tg = h_ref.shape[0]

    def start(r, _):
        for k in range(TOP_K):
            dst = dest_ref[0, 0, TOP_K * r + k]
            _row_copy(h_ref.at[pl.ds(r, 1)], xs_out.at[pl.ds(dst, 1)], sem).start(priority=k % 2)
        return 0

    lax.fori_loop(0, tg, start, 0, unroll=4)

    def wait(r, _):
        for k in range(TOP_K):
            _row_copy(h_ref.at[pl.ds(0, 1)], xs_out.at[pl.ds(0, 1)], sem).wait()
        return 0

    lax.fori_loop(0, tg, wait, 0)


def _dispatch(h, dest, xs):
    rows, d = h.shape
    tg = _row_tile(rows, GATHER_TILE)
    dest3 = dest.reshape(rows // tg, 1, TOP_K * tg)
    return pl.pallas_call(
        _dispatch_kernel,
        grid=(rows // tg,),
        in_specs=[
            pl.BlockSpec((1, 1, TOP_K * tg), lambda i: (i, 0, 0), memory_space=pltpu.SMEM),
            pl.BlockSpec((tg, d), lambda i: (i, 0)),
            pl.BlockSpec(memory_space=pl.ANY),
        ],
        out_specs=pl.BlockSpec(memory_space=pl.ANY),
        out_shape=jax.ShapeDtypeStruct(xs.shape, xs.dtype),
        scratch_shapes=[pltpu.SemaphoreType.DMA(())],
        input_output_aliases={2: 0},
        compiler_params=_params(("arbitrary",)),
        name="moe_dispatch",
    )(dest3, h, xs)


def _expert_kernel(te_ref, nu_ref, rows_ref, x_ref, wg_ref, wu_ref, wd_ref, y_ref, xb, acc,
                   *, sub):
    del te_ref
    j = pl.program_id(0)
    f = pl.program_id(1)
    te = x_ref.shape[0]

    @pl.when(j < nu_ref[0])
    def _():
        @pl.when(f == 0)
        def _():
            xb[...] = x_ref[...].astype(BF16)
            acc[...] = jnp.zeros_like(acc)

        wg = wg_ref[0, 0].astype(BF16)
        wu = wu_ref[0, 0].astype(BF16)
        wd = wd_ref[0, 0].astype(BF16)
        def swiglu(n_rows):
            h = xb[0:n_rows, :]
            gate = _mm(h, wg)
            act = (gate * jax.nn.sigmoid(gate)) * _mm(h, wu)
            acc[0:n_rows, :] += _mm(act.astype(BF16), wd)

        n_sub = te // sub
        for s in range(1, n_sub + 1):
            lo = (s - 1) * sub
            hi = s * sub if s < n_sub else te
            @pl.when((rows_ref[j] > lo) & (rows_ref[j] <= hi))
            def _():
                swiglu(s * sub)

        @pl.when(f == pl.num_programs(1) - 1)
        def _():
            y_ref[...] = acc[...]

    @pl.when((j >= nu_ref[0]) & (f == pl.num_programs(1) - 1))
    def _():
        y_ref[...] = jnp.zeros_like(y_ref)


def _experts(xs, tile_expert, n_used, tile_rows, wg, wu, wd, layer):
    p_pad, d = xs.shape
    dff = wg.shape[3]
    te = EXP_TILE
    tf = _lane_tile(dff, FF_TILE)
    n_f = dff // tf

    def jj(j, nu):
        return jnp.minimum(j, nu[0] - 1)

    def ff(j, f, nu):
        return jnp.where(j < nu[0], f, n_f - 1)

    return pl.pallas_call(
        functools.partial(_expert_kernel, sub=EXP_SUB),
        grid_spec=pltpu.PrefetchScalarGridSpec(
            num_scalar_prefetch=3,
            grid=(p_pad // te, n_f),
            in_specs=[
                pl.BlockSpec((te, d), lambda j, f, t, nu, nr: (jj(j, nu), 0)),
                pl.BlockSpec((1, 1, d, tf),
                             lambda j, f, t, nu, nr: (layer, t[jj(j, nu)], 0, ff(j, f, nu))),
                pl.BlockSpec((1, 1, d, tf),
                             lambda j, f, t, nu, nr: (layer, t[jj(j, nu)], 0, ff(j, f, nu))),
                pl.BlockSpec((1, 1, tf, d),
                             lambda j, f, t, nu, nr: (layer, t[jj(j, nu)], ff(j, f, nu), 0)),
            ],
            out_specs=pl.BlockSpec((te, d), lambda j, f, t, nu, nr: (j, 0)),
            scratch_shapes=[pltpu.VMEM((te, d), BF16), pltpu.VMEM((te, d), F32)],
        ),
        out_shape=jax.ShapeDtypeStruct((p_pad, d), F32),
        compiler_params=_params(("arbitrary", "arbitrary")),
        name="moe_experts",
    )(tile_expert, n_used, tile_rows, xs, wg, wu, wd)


def _combine_kernel(dest_ref, x_ref, info_ref, g_ref, ys, o_ref, buf, sem, *, final_norm):
    tg = x_ref.shape[0]

    def start(r, _):
        for k in range(TOP_K):
            src = dest_ref[0, 0, TOP_K * r + k]
            _row_copy(ys.at[pl.ds(src, 1)], buf.at[k, pl.ds(r, 1)], sem).start(priority=k % 2)
        return 0

    lax.fori_loop(0, tg, start, 0, unroll=4)

    def wait(r, _):
        for k in range(TOP_K):
            _row_copy(ys.at[pl.ds(0, 1)], buf.at[0, pl.ds(0, 1)], sem).wait()
        return 0

    lax.fori_loop(0, tg, wait, 0)
    info = info_ref[...]
    out = x_ref[...] + info[:, 2:3] * buf[0]
    out = out + info[:, 3:4] * buf[1]
    if final_norm:
        out = _rms(out, g_ref[...])
    o_ref[...] = out


def _combine(x, info, dest, ys, final_g):
    rows, d = x.shape
    lanes = info.shape[1]
    tg = _row_tile(rows, GATHER_TILE)
    dest3 = dest.reshape(rows // tg, 1, TOP_K * tg)
    final_norm = final_g is not None
    g = final_g if final_norm else jnp.ones((1, d), F32)
    return pl.pallas_call(
        functools.partial(_combine_kernel, final_norm=final_norm),
        grid=(rows // tg,),
        in_specs=[
            pl.BlockSpec((1, 1, TOP_K * tg), lambda i: (i, 0, 0), memory_space=pltpu.SMEM),
            pl.BlockSpec((tg, d), lambda i: (i, 0)),
            pl.BlockSpec((tg, lanes), lambda i: (i, 0)),
            pl.BlockSpec((1, d), lambda i: (0, 0)),
            pl.BlockSpec(memory_space=pl.ANY),
        ],
        out_specs=pl.BlockSpec((tg, d), lambda i: (i, 0)),
        out_shape=jax.ShapeDtypeStruct((rows, d), F32),
        scratch_shapes=[pltpu.VMEM((TOP_K, tg, d), F32), pltpu.SemaphoreType.DMA(())],
        compiler_params=_params(("arbitrary",)),
        name="moe_combine",
    )(dest3, x, info, g, ys)


def _moe(xs_list, g, rw, wg, wu, wd, layer, final_g):
    n_exp = rw.shape[1]
    d = xs_list[0].shape[1]
    te = EXP_TILE
    cnt = jnp.zeros((8, V7X_LANES), F32)
    hs, infos = [], []
    for x in xs_list:
        h, info, cnt = _router(x, g, rw, cnt)
        hs.append(h)
        infos.append(info)
    pairs = TOP_K * sum(x.shape[0] for x in xs_list)
    n_tiles = -(-pairs // te) + n_exp
    counts = cnt[0, :n_exp].astype(I32)
    padded = ((counts + te - 1) // te) * te
    ends = jnp.cumsum(padded)
    offs = ends - padded
    n_used = (ends[-1] // te).astype(I32).reshape(1)
    tile_expert = jnp.searchsorted(ends, jnp.arange(n_tiles, dtype=I32) * te, side="right")
    tile_expert = jnp.minimum(tile_expert, n_exp - 1).astype(I32)
    tile_start = jnp.arange(n_tiles, dtype=I32) * te
    tile_rows = jnp.clip(counts[tile_expert] - (tile_start - offs[tile_expert]), 0, te).astype(I32)
    dests = [offs[info[:, 0:TOP_K].astype(I32)] + info[:, 4:4 + TOP_K].astype(I32)
             for info in infos]
    xs = jnp.zeros((n_tiles * te, d), F32)
    for h, dest in zip(hs, dests):
        xs = _dispatch(h, dest, xs)
    ys = _experts(xs, tile_expert, n_used, tile_rows, wg, wu, wd, layer)
    return [_combine(x, info, dest, ys, final_g) for x, info, dest in zip(xs_list, infos, dests)]


def _norm_kernel(x_ref, g_ref, o_ref):
    o_ref[...] = _rms(x_ref[...], g_ref[...])


def _final_norm(x, g):
    rows, d = x.shape
    tm = _row_tile(rows, ROW_TILE)
    row = pl.BlockSpec((tm, d), lambda i: (i, 0))
    return pl.pallas_call(
        _norm_kernel, grid=(rows // tm,),
        in_specs=[row, pl.BlockSpec((1, d), lambda i: (0, 0))], out_specs=row,
        out_shape=jax.ShapeDtypeStruct((rows, d), F32),
        compiler_params=_params(("arbitrary",)), name="final_norm",
    )(x, g)


def kernel(x_prompt, x_sample, cache_k, cache_v, state_conv, page_table, meta_tokens, norm_mix,
           norm_ffn, norm_final, conv_w_in, conv_w, conv_w_out, sb_w_qkv, sb_w_o, sb_bias,
           ffn_w_gate, ffn_w_up, ffn_w_down, moe_router, moe_w_gate, moe_w_up, moe_w_down):
    n_p, seq, d = x_prompt.shape
    n_s = x_sample.shape[0]
    hd = d // N_HEADS
    depth = norm_mix.shape[0]
    t = seq + N_META
    meta = jnp.broadcast_to(meta_tokens[None].astype(x_prompt.dtype), (n_p, N_META, d))
    xp = jnp.concatenate([meta, x_prompt], axis=1).reshape(n_p * t, d)
    xs = x_sample.reshape(n_s, d)
    row = lambda a: a.reshape(1, d)
    k_p, v_p, k_s, v_s, conv_p, conv_s = [], [], [], [], [], []
    for i in range(depth):
        j = i // 2
        last = i == depth - 1
        g_mix = row(norm_mix[i])
        g_ffn = row(norm_ffn[i])
        if i % 2 == 0:
            w_in = conv_w_in[j].astype(BF16)
            w_out = conv_w_out[j].astype(BF16)
            xp, cp = _conv_mixer_prompt(xp, g_mix, w_in, conv_w[j], w_out, t)
            xs, cs = _conv_mixer_sample(xs, state_conv[j], g_mix, w_in, conv_w[j], w_out)
            conv_p.append(cp)
            conv_s.append(cs)
            wg = ffn_w_gate[j].astype(BF16)
            wu = ffn_w_up[j].astype(BF16)
            wd = ffn_w_down[j].astype(BF16)
            xp = _ffn(xp, g_ffn, wg, wu, wd)
            xs = _ffn(xs, g_ffn, wg, wu, wd)
            if last:
                xp = _final_norm(xp, row(norm_final))
                xs = _final_norm(xs, row(norm_final))
        else:
            w_qkv = sb_w_qkv[j].astype(BF16)
            w_o = sb_w_o[j].astype(BF16)
            qp, kp, vp, k_att, v_att = _qkv_prompt(xp, g_mix, w_qkv, n_p, t)
            qs, ks, vs = _qkv(xs, g_mix, w_qkv)
            k_p.append(kp.reshape(n_p, t, N_HEADS, hd))
            v_p.append(vp.reshape(n_p, t, N_HEADS, hd))
            k_s.append(ks.reshape(n_s, 1, N_HEADS, hd))
            v_s.append(vs.reshape(n_s, 1, N_HEADS, hd))
            op = _sb_attention_prompt(qp, k_att, v_att, sb_bias[j])
            os_ = _sb_attention_sample(qs, cache_k, cache_v, j, page_table, sb_bias[j])
            xp = _proj_residual(op, w_o, xp)
            xs = _proj_residual(os_, w_o, xs)
            xp, xs = _moe([xp, xs], g_ffn, moe_router[j], moe_w_gate, moe_w_up, moe_w_down, j,
                          row(norm_final) if last else None)
    y_prompt = xp.reshape(n_p, t, d)[:, N_META:]
    y_sample = xs.reshape(n_s, 1, d)
    return (y_prompt, y_sample, jnp.stack(k_p), jnp.stack(v_p), jnp.stack(k_s), jnp.stack(v_s),
            jnp.stack(conv_p), jnp.stack(conv_s))
```

```python
import functools

import jax
import jax.numpy as jnp
from jax import lax
from jax.experimental import pallas as pl
from jax.experimental.pallas import tpu as pltpu

F32 = jnp.float32
BF16 = jnp.bfloat16
I32 = jnp.int32

EPS = 1e-6
LOG2E = 1.4426950408889634
N_HEADS = 16
N_META = 16
TOP_K = 2
CONV_WIDTH = 3

V7X_SUBLANES = 8
V7X_LANES = 128
V7X_VMEM_BYTES = 64 * 1024 * 1024
VMEM_LIMIT = 56 * 1024 * 1024

ROW_TILE = 912
CONV_TILE = 432
FF_TILE = 512
ATT_TILE = 256
ATT_HEADS = 4
ATT_BLOCKS = 8
QKV_TILE = 1024
DECODE_PAGES = 8
EXP_TILE = 1024
EXP_SUB = 512
GATHER_TILE = 304


def _params(sem):
    return pltpu.CompilerParams(dimension_semantics=sem, vmem_limit_bytes=VMEM_LIMIT)


def _row_tile(rows, target):
    best = None
    for t in range(16, min(rows, target) + 1, 16):
        if rows % t == 0:
            best = t
    assert best is not None, rows
    return best


def _lane_tile(cols, target):
    best = None
    for t in range(V7X_LANES, min(cols, target) + 1, V7X_LANES):
        if cols % t == 0:
            best = t
    assert best is not None, cols
    return best


def _rms(x, g):
    y = x * lax.rsqrt(jnp.mean(x * x, axis=-1, keepdims=True) + EPS)
    return y * g


def _softplus(z):
    return jnp.maximum(z, 0.0) + jnp.log(1.0 + jnp.exp(-jnp.abs(z)))


def _mm(a, b):
    return jnp.dot(a, b, preferred_element_type=F32)


def _conv_prompt_kernel(x_ref, g_ref, win_ref, cw_ref, wout_ref, o_ref, st_ref, ubuf,
                        *, tiles_per_seq):
    i = pl.program_id(0)
    tm, d = x_ref.shape

    @pl.when(i % tiles_per_seq == 0)
    def _():
        ubuf[0:8, :] = jnp.zeros((8, d), F32)

    x = x_ref[...]
    hb = _rms(x, g_ref[...]).astype(BF16)
    b_gate = _mm(hb, win_ref[:, 0:d])
    u = _mm(hb, win_ref[:, d:2 * d]) * _mm(hb, win_ref[:, 2 * d:3 * d])
    ubuf[8:8 + tm, :] = u
    conv = ubuf[6:6 + tm, :] * cw_ref[0:1, :]
    conv = conv + ubuf[7:7 + tm, :] * cw_ref[1:2, :]
    conv = conv + u * cw_ref[2:3, :]
    y = _mm((b_gate * conv).astype(BF16), wout_ref[...])
    o_ref[...] = x + y
    last = ubuf[tm:tm + 8, :]
    ubuf[0:8, :] = last
    st_ref[0] = last


def _conv_mixer_prompt(x, g, w_in, cw, w_out, seq_len):
    rows, d = x.shape
    tm = _row_tile(seq_len, CONV_TILE)
    tiles_per_seq = seq_len // tm
    n_seq = rows // seq_len
    out, st = pl.pallas_call(
        functools.partial(_conv_prompt_kernel, tiles_per_seq=tiles_per_seq),
        grid=(rows // tm,),
        in_specs=[
            pl.BlockSpec((tm, d), lambda i: (i, 0)),
            pl.BlockSpec((1, d), lambda i: (0, 0)),
            pl.BlockSpec((d, 3 * d), lambda i: (0, 0)),
            pl.BlockSpec((CONV_WIDTH, d), lambda i: (0, 0)),
            pl.BlockSpec((d, d), lambda i: (0, 0)),
        ],
        out_specs=[
            pl.BlockSpec((tm, d), lambda i: (i, 0)),
            pl.BlockSpec((1, 8, d), lambda i: (i // tiles_per_seq, 0, 0)),
        ],
        out_shape=[jax.ShapeDtypeStruct((rows, d), F32),
                   jax.ShapeDtypeStruct((n_seq, 8, d), F32)],
        scratch_shapes=[pltpu.VMEM((tm + 8, d), F32)],
        compiler_params=_params(("arbitrary",)),
        name="conv_mixer_prompt",
    )(x, g, w_in, cw, w_out)
    return out, st[:, 8 - (CONV_WIDTH - 1):, :]


def _conv_sample_kernel(x_ref, s0_ref, s1_ref, g_ref, win_ref, cw_ref, wout_ref, o_ref, u_ref):
    d = x_ref.shape[1]
    x = x_ref[...]
    hb = _rms(x, g_ref[...]).astype(BF16)
    b_gate = _mm(hb, win_ref[:, 0:d])
    u = _mm(hb, win_ref[:, d:2 * d]) * _mm(hb, win_ref[:, 2 * d:3 * d])
    conv = s0_ref[...] * cw_ref[0:1, :]
    conv = conv + s1_ref[...] * cw_ref[1:2, :]
    conv = conv + u * cw_ref[2:3, :]
    y = _mm((b_gate * conv).astype(BF16), wout_ref[...])
    o_ref[...] = x + y
    u_ref[...] = u


def _conv_mixer_sample(x, state, g, w_in, cw, w_out):
    n, d = x.shape
    full = lambda shape: pl.BlockSpec(shape, lambda i: (0,) * len(shape))
    out, u = pl.pallas_call(
        _conv_sample_kernel,
        grid=(1,),
        in_specs=[full((n, d)), full((n, d)), full((n, d)), full((1, d)),
                  full((d, 3 * d)), full((CONV_WIDTH, d)), full((d, d))],
        out_specs=[full((n, d)), full((n, d))],
        out_shape=[jax.ShapeDtypeStruct((n, d), F32)] * 2,
        compiler_params=_params(("arbitrary",)),
        name="conv_mixer_sample",
    )(x, state[:, 0], state[:, 1], g, w_in, cw, w_out)
    return out, jnp.stack([state[:, 1], u], axis=1)


def _ffn_kernel(x_ref, g_ref, wg_ref, wu_ref, wd_ref, o_ref, hb, acc):
    f = pl.program_id(1)

    @pl.when(f == 0)
    def _():
        x = x_ref[...]
        hb[...] = _rms(x, g_ref[...]).astype(BF16)
        acc[...] = x

    h = hb[...]
    gate = _mm(h, wg_ref[...])
    act = (gate * jax.nn.sigmoid(gate)) * _mm(h, wu_ref[...])
    acc[...] += _mm(act.astype(BF16), wd_ref[...])

    @pl.when(f == pl.num_programs(1) - 1)
    def _():
        o_ref[...] = acc[...]


def _ffn(x, g, wg, wu, wd):
    rows, d = x.shape
    dff = wg.shape[1]
    tm = _row_tile(rows, ROW_TILE)
    tf = _lane_tile(dff, FF_TILE)
    return pl.pallas_call(
        _ffn_kernel,
        grid=(rows // tm, dff // tf),
        in_specs=[
            pl.BlockSpec((tm, d), lambda i, f: (i, 0)),
            pl.BlockSpec((1, d), lambda i, f: (0, 0)),
            pl.BlockSpec((d, tf), lambda i, f: (0, f)),
            pl.BlockSpec((d, tf), lambda i, f: (0, f)),
            pl.BlockSpec((tf, d), lambda i, f: (f, 0)),
        ],
        out_specs=pl.BlockSpec((tm, d), lambda i, f: (i, 0)),
        out_shape=jax.ShapeDtypeStruct((rows, d), F32),
        scratch_shapes=[pltpu.VMEM((tm, d), BF16), pltpu.VMEM((tm, d), F32)],
        compiler_params=_params(("arbitrary", "arbitrary")),
        name="ffn_dense",
    )(x, g, wg, wu, wd)


def _qkv_kernel(x_ref, g_ref, w_ref, q_ref, k_ref, v_ref, *, q_scale):
    d = x_ref.shape[1]
    hb = _rms(x_ref[...], g_ref[...]).astype(BF16)
    q_ref[...] = (_mm(hb, w_ref[:, 0:d]) * q_scale).astype(BF16)
    k_ref[...] = _mm(hb, w_ref[:, d:2 * d])
    v_ref[...] = _mm(hb, w_ref[:, 2 * d:3 * d])


def _qkv(x, g, w):
    rows, d = x.shape
    tm = _row_tile(rows, ROW_TILE)
    q_scale = float((d // N_HEADS) ** -0.5)
    row = pl.BlockSpec((tm, d), lambda i: (i, 0))
    return pl.pallas_call(
        functools.partial(_qkv_kernel, q_scale=q_scale),
        grid=(rows // tm,),
        in_specs=[row, pl.BlockSpec((1, d), lambda i: (0, 0)),
                  pl.BlockSpec((d, 3 * d), lambda i: (0, 0))],
        out_specs=[row, row, row],
        out_shape=[jax.ShapeDtypeStruct((rows, d), BF16),
                   jax.ShapeDtypeStruct((rows, d), F32),
                   jax.ShapeDtypeStruct((rows, d), F32)],
        compiler_params=_params(("arbitrary",)),
        name="qkv_proj",
    )(x, g, w)


def _qkv_prompt_kernel(x_ref, g_ref, w_ref, q_ref, k_ref, v_ref, kp_ref, vt_ref, kbuf, vbuf,
                       *, q_scale, seq_len, blk):
    i = pl.program_id(1)
    tm, d = x_ref.shape[1], x_ref.shape[2]
    m = blk // V7X_SUBLANES
    hb = _rms(x_ref[0], g_ref[...]).astype(BF16)
    q_ref[0] = (_mm(hb, w_ref[:, 0:d]) * q_scale).astype(BF16)
    k = _mm(hb, w_ref[:, d:2 * d])
    v = _mm(hb, w_ref[:, 2 * d:3 * d])
    k_ref[0] = k
    v_ref[0] = v
    row = i * tm + lax.broadcasted_iota(I32, (tm, 1), 0)
    k = jnp.where(row < seq_len, k, 0.0)
    v = jnp.where(row < seq_len, v, 0.0)
    n_col = d // V7X_LANES
    for cc in range(n_col):
        kbuf[cc] = k[:, cc * V7X_LANES:(cc + 1) * V7X_LANES]
        vbuf[cc] = v[:, cc * V7X_LANES:(cc + 1) * V7X_LANES]

    def permuted(buf, c):
        cols = []
        for cc in range(n_col):
            slabs = [buf[cc, pl.ds(c * blk + r, V7X_SUBLANES, stride=m), :] for r in range(m)]
            cols.append(jnp.concatenate(slabs, axis=0))
        return jnp.concatenate(cols, axis=1)

    for c in range(tm // blk):
        kp_ref[0, c * blk:(c + 1) * blk, :] = permuted(kbuf, c).astype(BF16)
        vt_ref[0, :, c * blk:(c + 1) * blk] = permuted(vbuf, c).T.astype(BF16)


def _qkv_prompt(x, g, w, n_seq, seq_len):
    d = x.shape[1]
    blk = ATT_TILE
    t_pad = -(-seq_len // blk) * blk
    tm = min(QKV_TILE, t_pad)
    n_tiles = -(-seq_len // tm)
    q_scale = float((d // N_HEADS) ** -0.5) * LOG2E
    row = pl.BlockSpec((1, tm, d), lambda b, i: (b, i, 0))
    seq = lambda dt: jax.ShapeDtypeStruct((n_seq, seq_len, d), dt)
    return pl.pallas_call(
        functools.partial(_qkv_prompt_kernel, q_scale=q_scale, seq_len=seq_len, blk=blk),
        grid=(n_seq, n_tiles),
        in_specs=[row, pl.BlockSpec((1, d), lambda b, i: (0, 0)),
                  pl.BlockSpec((d, 3 * d), lambda b, i: (0, 0))],
        out_specs=[row, row, row, row, pl.BlockSpec((1, d, tm), lambda b, i: (b, 0, i))],
        out_shape=[seq(BF16), seq(F32), seq(F32),
                   jax.ShapeDtypeStruct((n_seq, t_pad, d), BF16),
                   jax.ShapeDtypeStruct((n_seq, d, t_pad), BF16)],
        scratch_shapes=[pltpu.VMEM((d // V7X_LANES, tm, V7X_LANES), F32)] * 2,
        compiler_params=_params(("arbitrary", "arbitrary")),
        name="qkv_proj_prompt",
    )(x.reshape(n_seq, seq_len, d), g, w)


def _proj_residual_kernel(o_ref, w_ref, x_ref, y_ref):
    y_ref[...] = x_ref[...] + _mm(o_ref[...], w_ref[...])


def _proj_residual(o, w, x):
    rows, d = x.shape
    tm = _row_tile(rows, ROW_TILE)
    row = pl.BlockSpec((tm, d), lambda i: (i, 0))
    return pl.pallas_call(
        _proj_residual_kernel,
        grid=(rows // tm,),
        in_specs=[row, pl.BlockSpec((d, d), lambda i: (0, 0)), row],
        out_specs=row,
        out_shape=jax.ShapeDtypeStruct((rows, d), F32),
        compiler_params=_params(("arbitrary",)),
        name="attn_out_proj",
    )(o, w, x)


def _tree_sum(xs):
    xs = list(xs)
    while len(xs) > 1:
        odd = [xs[-1]] if len(xs) % 2 else []
        xs = [xs[i] + xs[i + 1] for i in range(0, len(xs) - 1, 2)] + odd
    return xs[0]


def _neg_abs(z):
    bits = lax.bitcast_convert_type(z, jnp.uint32) | jnp.uint32(0x80000000)
    return lax.bitcast_convert_type(bits, F32)


def _sb_prompt_kernel(bias_ref, q_ref, k_ref, vt_ref, o_ref, carry, acc, zbuf,
                      *, tile, heads, hd):
    hg = pl.program_id(1)
    qi = pl.program_id(2)
    m = tile // V7X_SUBLANES
    width = heads * hd
    q_all = q_ref[0].astype(F32)
    q_lane = lax.broadcasted_iota(I32, (tile, width), 1)
    qs = [jnp.where((q_lane >= g * hd) & (q_lane < (g + 1) * hd), q_all, 0.0).astype(BF16)
          for g in range(heads)]
    biases = [bias_ref[hg * heads + g] for g in range(heads)]
    lane = lax.broadcasted_iota(I32, (V7X_SUBLANES, tile), 1)
    sub = lax.broadcasted_iota(I32, (V7X_SUBLANES, tile), 0)
    diag = lane - m * sub
    carry[...] = jnp.zeros_like(carry)
    acc[...] = jnp.zeros_like(acc)

    def blocks(lo, n, masked):
        start = lo * tile if isinstance(lo, int) else pl.multiple_of(lo * tile, tile)
        k_cat = k_ref[0, pl.ds(start, n * tile), :]
        for g in range(heads):
            s = lax.dot_general(k_cat, qs[g], (((1,), (1,)), ((), ())),
                                preferred_element_type=F32)
            zbuf[g, 0:n * tile, :] = s + biases[g]
        for g in range(heads):
            c = carry[g]
            out = acc[g]
            for u in reversed(range(n)):
                base = u * tile
                st = start + base
                if not isinstance(st, int):
                    st = pl.multiple_of(st, tile)
                zs = [zbuf[g, base + 8 * r:base + 8 * r + 8, :] for r in range(m)]
                sps = []
                for r in range(m):
                    z = zs[r]
                    spr = jnp.maximum(z, 0.0) + jnp.log(1.0 + jnp.exp2(_neg_abs(z))) * LOG2E
                    if masked:
                        spr = jnp.where(diag > r, spr, 0.0)
                    sps.append(spr)
                tot = _tree_sum(sps)
                off = c
                for j in range(1, V7X_SUBLANES):
                    rowj = jnp.broadcast_to(tot[j:j + 1], tot.shape)
                    off = off + jnp.where(sub < j, rowj, 0.0)
                run = off
                outs = [None] * m
                for r in reversed(range(m)):
                    run = run + sps[r]
                    a = jnp.exp2(zs[r] - run)
                    if masked:
                        a = jnp.where(diag > r, a, 0.0)
                    outs[r] = a
                a_full = jnp.concatenate(outs, axis=0).astype(BF16)
                vt = vt_ref[0, g * hd:(g + 1) * hd, pl.ds(st, tile)]
                out = out + _mm(vt, a_full)
                c = jnp.broadcast_to(off[0:1] + tot[0:1], tot.shape)
            carry[g] = c
            acc[g] = out

    blocks(qi, 1, True)
    nb = ATT_BLOCKS

    def body(j, c):
        blocks(qi - nb * (j + 1), nb, False)
        return c

    lax.fori_loop(0, qi // nb, body, 0)
    rem = qi % nb
    size = nb // 2
    while size >= 1:
        @pl.when((rem & size) != 0)
        def _(size=size):
            blocks(rem & (size - 1), size, False)
        size //= 2
    pairs = []
    for g in range(0, heads, 2):
        both = jnp.concatenate([acc[g], acc[g + 1]], axis=0)
        pairs.append(both.T)
    o_ref[0] = jnp.concatenate(pairs, axis=1).astype(o_ref.dtype)


def _sb_attention_prompt(q, kp, vt, bias):
    n_seq, seq_len, d = q.shape
    hd = d // N_HEADS
    tile = ATT_TILE
    heads = ATT_HEADS
    width = heads * hd
    t_pad = kp.shape[1]
    out = pl.pallas_call(
        functools.partial(_sb_prompt_kernel, tile=tile, heads=heads, hd=hd),
        grid=(n_seq, N_HEADS // heads, t_pad // tile),
        in_specs=[
            pl.BlockSpec(memory_space=pltpu.SMEM),
            pl.BlockSpec((1, tile, width), lambda b, h, i: (b, i, h)),
            pl.BlockSpec((1, t_pad, width), lambda b, h, i: (b, 0, h)),
            pl.BlockSpec((1, width, t_pad), lambda b, h, i: (b, h, 0)),
        ],
        out_specs=pl.BlockSpec((1, tile, width), lambda b, h, i: (b, i, h)),
        out_shape=jax.ShapeDtypeStruct((n_seq, seq_len, d), BF16),
        scratch_shapes=[pltpu.VMEM((heads, V7X_SUBLANES, tile), F32),
                        pltpu.VMEM((heads, hd, tile), F32),
                        pltpu.VMEM((heads, ATT_BLOCKS * tile, tile), F32)],
        compiler_params=_params(("arbitrary", "arbitrary", "arbitrary")),
        name="sb_attention_prompt",
    )(bias.astype(F32) * LOG2E, q, kp, vt)
    return out.reshape(n_seq * seq_len, d)


def _sb_decode_kernel(pt_ref, bias_ref, q_ref, tri_ref, *refs, n_pg):
    del pt_ref
    k_refs, v_refs = refs[:n_pg], refs[n_pg:2 * n_pg]
    o_ref, carry, acc = refs[2 * n_pg:]
    p = pl.program_id(1)
    nh = k_refs[0].shape[2]

    @pl.when(p == 0)
    def _():
        carry[...] = jnp.zeros_like(carry)
        acc[...] = jnp.zeros_like(acc)

    tri = tri_ref[...]
    c = carry[...]
    weights = []
    for u in range(n_pg):
        rows = [jnp.sum(k_refs[u][0, 0, h] * q_ref[0, h], axis=0, keepdims=True)
                for h in range(nh)]
        z = jnp.concatenate(rows, axis=0) + bias_ref[...]
        sp = _softplus(z)
        hi = sp.astype(BF16)
        r1 = sp - hi.astype(F32)
        mid = r1.astype(BF16)
        lo = (r1 - mid.astype(F32)).astype(BF16)
        after = _mm(hi, tri) + _mm(mid, tri) + _mm(lo, tri)
        weights.append(jnp.exp((z - sp) - (after + c)))
        c = c + jnp.broadcast_to(after[:, 0:1] + sp[:, 0:1], c.shape)
    carry[...] = c
    for h in range(nh):
        acc[h] += _tree_sum([v_refs[u][0, 0, h] * weights[u][h:h + 1, :] for u in range(n_pg)])

    @pl.when(p == pl.num_programs(1) - 1)
    def _():
        o_ref[0] = jnp.sum(acc[...], axis=2).astype(o_ref.dtype)


def _sb_attention_sample(q, cache_k, cache_v, layer, page_table, bias):
    n, d = q.shape
    nh = N_HEADS
    hd = d // nh
    n_pages = page_table.shape[1]
    ps = cache_k.shape[2]
    ck = jnp.transpose(cache_k, (0, 1, 3, 4, 2))
    cv = jnp.transpose(cache_v, (0, 1, 3, 4, 2))
    tri = (jnp.arange(ps)[:, None] > jnp.arange(ps)[None, :]).astype(BF16)
    bias_rep = jnp.broadcast_to(bias.astype(F32)[:, None], (nh, ps))
    q_rep = jnp.broadcast_to(q.astype(F32).reshape(n, nh, hd, 1), (n, nh, hd, ps))

    n_pg = DECODE_PAGES if n_pages % DECODE_PAGES == 0 else 1

    def page(u):
        return pl.BlockSpec(
            (1, 1, nh, hd, ps),
            lambda i, p, pt: (layer, pt[i, n_pages - 1 - (p * n_pg + u)], 0, 0, 0))

    const2 = lambda i, p, pt: (0, 0)
    out = pl.pallas_call(
        functools.partial(_sb_decode_kernel, n_pg=n_pg),
        grid_spec=pltpu.PrefetchScalarGridSpec(
            num_scalar_prefetch=1,
            grid=(n, n_pages // n_pg),
            in_specs=[
                pl.BlockSpec((nh, ps), const2),
                pl.BlockSpec((1, nh, hd, ps), lambda i, p, pt: (i, 0, 0, 0)),
                pl.BlockSpec((ps, ps), const2),
            ] + [page(u) for u in range(n_pg)] * 2,
            out_specs=pl.BlockSpec((1, nh, hd), lambda i, p, pt: (i, 0, 0)),
            scratch_shapes=[pltpu.VMEM((nh, ps), F32), pltpu.VMEM((nh, hd, ps), F32)],
        ),
        out_shape=jax.ShapeDtypeStruct((n, nh, hd), BF16),
        compiler_params=_params(("arbitrary", "arbitrary")),
        name="sb_attention_decode",
    )(page_table, bias_rep, q_rep, tri, *([ck] * n_pg), *([cv] * n_pg))
    return out.reshape(n, d)


def _router_kernel(x_ref, g_ref, rw_ref, tri_ref, cnt0_ref, h_ref, info_ref, cnt_ref, run,
                   *, n_exp):
    i = pl.program_id(0)
    tm = x_ref.shape[0]
    lanes = rw_ref.shape[1]

    @pl.when(i == 0)
    def _():
        run[...] = cnt0_ref[...]

    h = _rms(x_ref[...], g_ref[...])
    h_ref[...] = h
    logits = jnp.dot(h, rw_ref[...], preferred_element_type=F32,
                     precision=lax.Precision.HIGHEST)
    lane = lax.broadcasted_iota(I32, (tm, lanes), 1)
    valid = lane < n_exp
    lg = jnp.where(valid, logits, -jnp.inf)
    ex = jnp.exp(lg - jnp.max(lg, axis=1, keepdims=True))
    probs = ex / jnp.sum(ex, axis=1, keepdims=True)
    p1 = jnp.where(valid, probs, -1.0)
    m1 = jnp.max(p1, axis=1, keepdims=True)
    i1 = jnp.min(jnp.where(p1 == m1, lane, lanes), axis=1, keepdims=True)
    p2 = jnp.where(lane == i1, -1.0, p1)
    m2 = jnp.max(p2, axis=1, keepdims=True)
    i2 = jnp.min(jnp.where(p2 == m2, lane, lanes), axis=1, keepdims=True)
    den = m1 + m2
    g1 = m1 / den
    g2 = m2 / den
    hot = ((lane == i1) | (lane == i2)).astype(BF16)
    before = _mm(tri_ref[...], hot) + run[0:1, :]
    r1 = jnp.sum(jnp.where(lane == i1, before, 0.0), axis=1, keepdims=True)
    r2 = jnp.sum(jnp.where(lane == i2, before, 0.0), axis=1, keepdims=True)
    info = jnp.where(lane == 0, i1.astype(F32), 0.0)
    info = jnp.where(lane == 1, i2.astype(F32), info)
    info = jnp.where(lane == 2, g1, info)
    info = jnp.where(lane == 3, g2, info)
    info = jnp.where(lane == 4, r1, info)
    info = jnp.where(lane == 5, r2, info)
    info_ref[...] = info
    run[...] += _mm(jnp.ones((8, tm), BF16), hot)
    cnt_ref[...] = run[...]


def _router(x, g, rw, cnt0):
    rows, d = x.shape
    n_exp = rw.shape[1]
    lanes = V7X_LANES
    tm = _row_tile(rows, ROW_TILE)
    rw_pad = jnp.zeros((d, lanes), F32).at[:, :n_exp].set(rw)
    tri = (jnp.arange(tm)[None, :] < jnp.arange(tm)[:, None]).astype(BF16)
    row = pl.BlockSpec((tm, d), lambda i: (i, 0))
    const = lambda shape: pl.BlockSpec(shape, lambda i: (0, 0))
    return pl.pallas_call(
        functools.partial(_router_kernel, n_exp=n_exp),
        grid=(rows // tm,),
        in_specs=[row, const((1, d)), const((d, lanes)), const((tm, tm)), const((8, lanes))],
        out_specs=[row, pl.BlockSpec((tm, lanes), lambda i: (i, 0)), const((8, lanes))],
        out_shape=[jax.ShapeDtypeStruct((rows, d), F32),
                   jax.ShapeDtypeStruct((rows, lanes), F32),
                   jax.ShapeDtypeStruct((8, lanes), F32)],
        scratch_shapes=[pltpu.VMEM((8, lanes), F32)],
        compiler_params=_params(("arbitrary",)),
        name="moe_router",
    )(x, g, rw_pad, tri, cnt0)


def _row_copy(src, dst, sem):
    return pltpu.make_async_copy(src, dst, sem)


def _dispatch_kernel(dest_ref, h_ref, xs_in, xs_out, sem):
    del xs_in
    tg = h_ref.shape[0]

    def start(r, _):
        for k in range(TOP_K):
            dst = dest_ref[0, 0, TOP_K * r + k]
            _row_copy(h_ref.at[pl.ds(r, 1)], xs_out.at[pl.ds(dst, 1)], sem).start(priority=k % 2)
        return 0

    lax.fori_loop(0, tg, start, 0, unroll=4)

    def wait(r, _):
        for k in range(TOP_K):
            _row_copy(h_ref.at[pl.ds(0, 1)], xs_out.at[pl.ds(0, 1)], sem).wait()
        return 0

    lax.fori_loop(0, tg, wait, 0)


def _dispatch(h, dest, xs):
    rows, d = h.shape
    tg = _row_tile(rows, GATHER_TILE)
    dest3 = dest.reshape(rows // tg, 1, TOP_K * tg)
    return pl.pallas_call(
        _dispatch_kernel,
        grid=(rows // tg,),
        in_specs=[
            pl.BlockSpec((1, 1, TOP_K * tg), lambda i: (i, 0, 0), memory_space=pltpu.SMEM),
            pl.BlockSpec((tg, d), lambda i: (i, 0)),
            pl.BlockSpec(memory_space=pl.ANY),
        ],
        out_specs=pl.BlockSpec(memory_space=pl.ANY),
        out_shape=jax.ShapeDtypeStruct(xs.shape, xs.dtype),
        scratch_shapes=[pltpu.SemaphoreType.DMA(())],
        input_output_aliases={2: 0},
        compiler_params=_params(("arbitrary",)),
        name="moe_dispatch",
    )(dest3, h, xs)


def _expert_kernel(te_ref, nu_ref, rows_ref, x_ref, wg_ref, wu_ref, wd_ref, y_ref, xb, acc,
                   *, sub):
    del te_ref
    j = pl.program_id(0)
    f = pl.program_id(1)
    te = x_ref.shape[0]

    @pl.when(j < nu_ref[0])
    def _():
        @pl.when(f == 0)
        def _():
            xb[...] = x_ref[...].astype(BF16)
            acc[...] = jnp.zeros_like(acc)

        wg = wg_ref[0, 0].astype(BF16)
        wu = wu_ref[0, 0].astype(BF16)
        wd = wd_ref[0, 0].astype(BF16)
        def swiglu(n_rows):
            h = xb[0:n_rows, :]
            gate = _mm(h, wg)
            act = (gate * jax.nn.sigmoid(gate)) * _mm(h, wu)
            acc[0:n_rows, :] += _mm(act.astype(BF16), wd)

        n_sub = te // sub
        for s in range(1, n_sub + 1):
            lo = (s - 1) * sub
            hi = s * sub if s < n_sub else te
            @pl.when((rows_ref[j] > lo) & (rows_ref[j] <= hi))
            def _():
                swiglu(s * sub)

        @pl.when(f == pl.num_programs(1) - 1)
        def _():
            y_ref[...] = acc[...]

    @pl.when((j >= nu_ref[0]) & (f == pl.num_programs(1) - 1))
    def _():
        y_ref[...] = jnp.zeros_like(y_ref)


def _experts(xs, tile_expert, n_used, tile_rows, wg, wu, wd, layer):
    p_pad, d = xs.shape
    dff = wg.shape[3]
    te = EXP_TILE
    tf = _lane_tile(dff, FF_TILE)
    n_f = dff // tf

    def jj(j, nu):
        return jnp.minimum(j, nu[0] - 1)

    def ff(j, f, nu):
        return jnp.where(j < nu[0], f, n_f - 1)

    return pl.pallas_call(
        functools.partial(_expert_kernel, sub=EXP_SUB),
        grid_spec=pltpu.PrefetchScalarGridSpec(
            num_scalar_prefetch=3,
            grid=(p_pad // te, n_f),
            in_specs=[
                pl.BlockSpec((te, d), lambda j, f, t, nu, nr: (jj(j, nu), 0)),
                pl.BlockSpec((1, 1, d, tf),
                             lambda j, f, t, nu, nr: (layer, t[jj(j, nu)], 0, ff(j, f, nu))),
                pl.BlockSpec((1, 1, d, tf),
                             lambda j, f, t, nu, nr: (layer, t[jj(j, nu)], 0, ff(j, f, nu))),
                pl.BlockSpec((1, 1, tf, d),
                             lambda j, f, t, nu, nr: (layer, t[jj(j, nu)], ff(j, f, nu), 0)),
            ],
            out_specs=pl.BlockSpec((te, d), lambda j, f, t, nu, nr: (j, 0)),
            scratch_shapes=[pltpu.VMEM((te, d), BF16), pltpu.VMEM((te, d), F32)],
        ),
        out_shape=jax.ShapeDtypeStruct((p_pad, d), F32),
        compiler_params=_params(("arbitrary", "arbitrary")),
        name="moe_experts",
    )(tile_expert, n_used, tile_rows, xs, wg, wu, wd)


def _combine_kernel(dest_ref, x_ref, info_ref, g_ref, ys, o_ref, buf, sem, *, final_norm):
    tg = x_ref.shape[0]

    def start(r, _):
        for k in range(TOP_K):
            src = dest_ref[0, 0, TOP_K * r + k]
            _row_copy(ys.at[pl.ds(src, 1)], buf.at[k, pl.ds(r, 1)], sem).start(priority=k % 2)
        return 0

    lax.fori_loop(0, tg, start, 0, unroll=4)

    def wait(r, _):
        for k in range(TOP_K):
            _row_copy(ys.at[pl.ds(0, 1)], buf.at[0, pl.ds(0, 1)], sem).wait()
        return 0

    lax.fori_loop(0, tg, wait, 0)
    info = info_ref[...]
    out = x_ref[...] + info[:, 2:3] * buf[0]
    out = out + info[:, 3:4] * buf[1]
    if final_norm:
        out = _rms(out, g_ref[...])
    o_ref[...] = out


def _combine(x, info, dest, ys, final_g):
    rows, d = x.shape
    lanes = info.shape[1]
    tg = _row_tile(rows, GATHER_TILE)
    dest3 = dest.reshape(rows // tg, 1, TOP_K * tg)
    final_norm = final_g is not None
    g = final_g if final_norm else jnp.ones((1, d), F32)
    return pl.pallas_call(
        functools.partial(_combine_kernel, final_norm=final_norm),
        grid=(rows // tg,),
        in_specs=[
            pl.BlockSpec((1, 1, TOP_K * tg), lambda i: (i, 0, 0), memory_space=pltpu.SMEM),
            pl.BlockSpec((tg, d), lambda i: (i, 0)),
            pl.BlockSpec((tg, lanes), lambda i: (i, 0)),
            pl.BlockSpec((1, d), lambda i: (0, 0)),
            pl.BlockSpec(memory_space=pl.ANY),
        ],
        out_specs=pl.BlockSpec((tg, d), lambda i: (i, 0)),
        out_shape=jax.ShapeDtypeStruct((rows, d), F32),
        scratch_shapes=[pltpu.VMEM((TOP_K, tg, d), F32), pltpu.SemaphoreType.DMA(())],
        compiler_params=_params(("arbitrary",)),
        name="moe_combine",
    )(dest3, x, info, g, ys)


def _moe(xs_list, g, rw, wg, wu, wd, layer, final_g):
    n_exp = rw.shape[1]
    d = xs_list[0].shape[1]
    te = EXP_TILE
    cnt = jnp.zeros((8, V7X_LANES), F32)
    hs, infos = [], []
    for x in xs_list:
        h, info, cnt = _router(x, g, rw, cnt)
        hs.append(h)
        infos.append(info)
    pairs = TOP_K * sum(x.shape[0] for x in xs_list)
    n_tiles = -(-pairs // te) + n_exp
    counts = cnt[0, :n_exp].astype(I32)
    padded = ((counts + te - 1) // te) * te
    ends = jnp.cumsum(padded)
    offs = ends - padded
    n_used = (ends[-1] // te).astype(I32).reshape(1)
    tile_expert = jnp.searchsorted(ends, jnp.arange(n_tiles, dtype=I32) * te, side="right")
    tile_expert = jnp.minimum(tile_expert, n_exp - 1).astype(I32)
    tile_start = jnp.arange(n_tiles, dtype=I32) * te
    tile_rows = jnp.clip(counts[tile_expert] - (tile_start - offs[tile_expert]), 0, te).astype(I32)
    dests = [offs[info[:, 0:TOP_K].astype(I32)] + info[:, 4:4 + TOP_K].astype(I32)
             for info in infos]
    xs = jnp.zeros((n_tiles * te, d), F32)
    for h, dest in zip(hs, dests):
        xs = _dispatch(h, dest, xs)
    ys = _experts(xs, tile_expert, n_used, tile_rows, wg, wu, wd, layer)
    return [_combine(x, info, dest, ys, final_g) for x, info, dest in zip(xs_list, infos, dests)]


def _norm_kernel(x_ref, g_ref, o_ref):
    o_ref[...] = _rms(x_ref[...], g_ref[...])


def _final_norm(x, g):
    rows, d = x.shape
    tm = _row_tile(rows, ROW_TILE)
    row = pl.BlockSpec((tm, d), lambda i: (i, 0))
    return pl.pallas_call(
        _norm_kernel, grid=(rows // tm,),
        in_specs=[row, pl.BlockSpec((1, d), lambda i: (0, 0))], out_specs=row,
        out_shape=jax.ShapeDtypeStruct((rows, d), F32),
        compiler_params=_params(("arbitrary",)), name="final_norm",
    )(x, g)


def kernel(x_prompt, x_sample, cache_k, cache_v, state_conv, page_table, meta_tokens, norm_mix,
           norm_ffn, norm_final, conv_w_in, conv_w, conv_w_out, sb_w_qkv, sb_w_o, sb_bias,
           ffn_w_gate, ffn_w_up, ffn_w_down, moe_router, moe_w_gate, moe_w_up, moe_w_down):
    n_p, seq, d = x_prompt.shape
    n_s = x_sample.shape[0]
    hd = d // N_HEADS
    depth = norm_mix.shape[0]
    t = seq + N_META
    meta = jnp.broadcast_to(meta_tokens[None].astype(x_prompt.dtype), (n_p, N_META, d))
    xp = jnp.concatenate([meta, x_prompt], axis=1).reshape(n_p * t, d)
    xs = x_sample.reshape(n_s, d)
    row = lambda a: a.reshape(1, d)
    k_p, v_p, k_s, v_s, conv_p, conv_s = [], [], [], [], [], []
    for i in range(depth):
        j = i // 2
        last = i == depth - 1
        g_mix = row(norm_mix[i])
        g_ffn = row(norm_ffn[i])
        if i % 2 == 0:
            w_in = conv_w_in[j].astype(BF16)
            w_out = conv_w_out[j].astype(BF16)
            xp, cp = _conv_mixer_prompt(xp, g_mix, w_in, conv_w[j], w_out, t)
            xs, cs = _conv_mixer_sample(xs, state_conv[j], g_mix, w_in, conv_w[j], w_out)
            conv_p.append(cp)
            conv_s.append(cs)
            wg = ffn_w_gate[j].astype(BF16)
            wu = ffn_w_up[j].astype(BF16)
            wd = ffn_w_down[j].astype(BF16)
            xp = _ffn(xp, g_ffn, wg, wu, wd)
            xs = _ffn(xs, g_ffn, wg, wu, wd)
            if last:
                xp = _final_norm(xp, row(norm_final))
                xs = _final_norm(xs, row(norm_final))
        else:
            w_qkv = sb_w_qkv[j].astype(BF16)
            w_o = sb_w_o[j].astype(BF16)
            qp, kp, vp, k_att, v_att = _qkv_prompt(xp, g_mix, w_qkv, n_p, t)
            qs, ks, vs = _qkv(xs, g_mix, w_qkv)
            k_p.append(kp.reshape(n_p, t, N_HEADS, hd))
            v_p.append(vp.reshape(n_p, t, N_HEADS, hd))
            k_s.append(ks.reshape(n_s, 1, N_HEADS, hd))
            v_s.append(vs.reshape(n_s, 1, N_HEADS, hd))
            op = _sb_attention_prompt(qp, k_att, v_att, sb_bias[j])
            os_ = _sb_attention_sample(qs, cache_k, cache_v, j, page_table, sb_bias[j])
            xp = _proj_residual(op, w_o, xp)
            xs = _proj_residual(os_, w_o, xs)
            xp, xs = _moe([xp, xs], g_ffn, moe_router[j], moe_w_gate, moe_w_up, moe_w_down, j,
                          row(norm_final) if last else None)
    y_prompt = xp.reshape(n_p, t, d)[:, N_META:]
    y_sample = xs.reshape(n_s, 1, d)
    return (y_prompt, y_sample, jnp.stack(k_p), jnp.stack(v_p), jnp.stack(k_s), jnp.stack(v_s),
            jnp.stack(conv_p), jnp.stack(conv_s))
```
